```python
import math
import jax, jax.numpy as jnp
from jax import lax
import numpy as np

D_MODEL = 1024
BATCH = 8
SEQ = 4096
DEPTH = 4
DEC_BATCH = 32
DEC_SEQ = 32
PAST_LEN = 1024

CHUNK = 64
N_BRANCH = 4
W_BR = D_MODEL
W_A = W_BR
W_B = W_BR
W_C = W_BR
H_D = 8
DK = D_MODEL // H_D
DV = W_BR // H_D
W_QK = H_D * DK
W_D = H_D * DV
K_A = 3
K_B = 4
H_B = 8
BW_B = W_B // H_B
K_C = 31
K_D = 4
LRU_C = 8.0
ALPHA = (2 * DEPTH) ** 0.25
BETA_INIT = (8 * DEPTH) ** -0.25
LN_EPS = 1e-5
RMS_EPS = 1e-6
L2_EPS = 1e-6

OFF_A = 0
OFF_B = OFF_A + 4 * W_A
OFF_C = OFF_B + 2 * W_B
OFF_D = OFF_C + 3 * W_C
OFF_DZ = OFF_D + 2 * W_QK + W_D
OFF_DA = OFF_DZ + W_D
OFF_DB = OFF_DA + H_D
OFF_G = OFF_DB + H_D
N_IN = OFF_G + N_BRANCH * D_MODEL

kernel_name = 'hybrid_streaming_encoder_step'

f32 = jnp.float32


def layer_norm(x, g, b):
    xf = x.astype(f32)
    mu = jnp.mean(xf, axis=-1, keepdims=True)
    var = jnp.mean(jnp.square(xf - mu), axis=-1, keepdims=True)
    return ((xf - mu) * lax.rsqrt(var + LN_EPS) * g + b).astype(x.dtype)


def l2norm(x):
    return x * lax.rsqrt(jnp.sum(x * x, axis=-1, keepdims=True) + L2_EPS)


def causal_dwconv(u, buf, w):
    xp = jnp.concatenate([buf.astype(u.dtype), u], axis=1)
    y = lax.conv_general_dilated(xp, w[:, None, :].astype(u.dtype), window_strides=(1,),
                                 padding='VALID', dimension_numbers=('NWC', 'WIO', 'NWC'),
                                 feature_group_count=u.shape[-1])
    return y, xp[:, xp.shape[1] - (w.shape[0] - 1):]


def rg_lru(x, h0, wx, bx, wa, ba, lam):
    bsz, l, w = x.shape
    xf = x.astype(f32)
    xh = xf.reshape(bsz, l, H_B, BW_B)
    gate_x = jax.nn.sigmoid(jnp.einsum('blhi,hij->blhj', xh, wx) + bx).reshape(bsz, l, w)
    gate_a = jax.nn.sigmoid(jnp.einsum('blhi,hij->blhj', xh, wa) + ba).reshape(bsz, l, w)
    log_a = -LRU_C * gate_a * jax.nn.softplus(-lam.astype(f32))
    a = jnp.exp(log_a)
    bterm = jnp.sqrt(-jnp.expm1(2.0 * log_a)) * gate_x * xf
    bterm = bterm.at[:, 0].add(a[:, 0] * h0.astype(f32))

    def combine(e1, e2):
        a1, b1 = e1
        a2, b2 = e2
        return a1 * a2, a2 * b1 + b2

    _, h = lax.associative_scan(combine, (a, bterm), axis=1)
    return h.astype(x.dtype), h[:, -1].astype(x.dtype)


def gated_delta_chunked(q, k, v, g, beta, s0):
    bsz, l, h, dk = q.shape
    dv = v.shape[-1]
    c = min(CHUNK, l)
    n = -(-l // c)
    pad = n * c - l
    if pad:
        p4 = ((0, 0), (0, pad), (0, 0), (0, 0))
        q, k, v = jnp.pad(q, p4), jnp.pad(k, p4), jnp.pad(v, p4)
        g, beta = jnp.pad(g, p4[:3]), jnp.pad(beta, p4[:3])

    def blocks(t):
        t = t.reshape((bsz, n, c, h) + t.shape[3:])
        return jnp.moveaxis(t, (1, 3), (0, 2))

    q = blocks(q) * (dk ** -0.5)
    k, v, g, beta = blocks(k), blocks(v), blocks(g), blocks(beta)
    gc = jnp.cumsum(g, axis=-1)
    idx = jnp.arange(c)
    incl = idx[:, None] >= idx[None, :]
    strict = idx[:, None] > idx[None, :]
    decay = jnp.exp(jnp.where(incl, gc[..., :, None] - gc[..., None, :], -jnp.inf))
    kb = k * beta[..., None]
    a_mat = jnp.where(strict, jnp.einsum('nbhid,nbhjd->nbhij', kb, k) * decay, 0.0)
    t_mat = a_mat + jnp.eye(c, dtype=a_mat.dtype)
    u_base = lax.linalg.triangular_solve(t_mat, v * beta[..., None], left_side=True,
                                         lower=True, unit_diagonal=True)
    w_mat = lax.linalg.triangular_solve(t_mat, kb * jnp.exp(gc)[..., None], left_side=True,
                                        lower=True, unit_diagonal=True)
    p_mat = jnp.einsum('nbhid,nbhjd->nbhij', q, k) * decay
    q_dec = q * jnp.exp(gc)[..., None]
    k_end = k * jnp.exp(gc[..., -1:] - gc)[..., None]
    g_end = jnp.exp(gc[..., -1])

    def step(s, xs):
        u0, w_, qd, pm, ke, ge = xs
        u = u0 - jnp.einsum('bhcd,bhde->bhce', w_, s)
        o = jnp.einsum('bhcd,bhde->bhce', qd, s) + jnp.einsum('bhij,bhje->bhie', pm, u)
        s = s * ge[..., None, None] + jnp.einsum('bhcd,bhce->bhde', ke, u)
        return s, o

    s_fin, o = lax.scan(step, s0, (u_base, w_mat, q_dec, p_mat, k_end, g_end))
    o = jnp.moveaxis(o, (0, 2), (1, 3)).reshape(bsz, n * c, h, dv)[:, :l]
    return o, s_fin


def mixer_layer(x, states, p):
    conv_a, conv_b, lru, conv_c, conv_d, delta = states
    bsz, l, _ = x.shape
    proj = x @ p['w_in']

    def col(off, w):
        return proj[..., off:off + w]

    a_conv, new_conv_a = causal_dwconv(col(OFF_A + W_A, W_A) * col(OFF_A + 2 * W_A, W_A),
                                       conv_a, p['a_conv_w'])
    y_a = col(OFF_A, W_A) * a_conv * jax.nn.silu(col(OFF_A + 3 * W_A, W_A))

    b_xc, new_conv_b = causal_dwconv(col(OFF_B, W_B), conv_b, p['b_conv_w'])
    h, new_lru = rg_lru(b_xc + p['b_conv_b'], lru, p['b_wx'], p['b_bx'], p['b_wa'],
                        p['b_ba'], p['b_lambda'])
    y_b = h * jax.nn.silu(col(OFF_B + W_B, W_B))

    c_glu = col(OFF_C, W_C) * jax.nn.sigmoid(col(OFF_C + W_C, W_C))
    c_cv, new_conv_c = causal_dwconv(c_glu, conv_c, p['c_conv_w'])
    y_c = (jax.nn.silu(layer_norm(c_cv + p['c_conv_b'], p['c_ln_g'], p['c_ln_b']))
           * jax.nn.silu(col(OFF_C + 2 * W_C, W_C)))

    qkv, new_conv_d = causal_dwconv(col(OFF_D, 2 * W_QK + W_D), conv_d, p['d_conv_w'])
    qkv = jax.nn.silu(qkv.astype(f32))
    q = l2norm(qkv[..., :W_QK].reshape(bsz, l, H_D, DK))
    k = l2norm(qkv[..., W_QK:2 * W_QK].reshape(bsz, l, H_D, DK))
    v = qkv[..., 2 * W_QK:].reshape(bsz, l, H_D, DV)
    beta = jax.nn.sigmoid(col(OFF_DB, H_D).astype(f32))
    g = -jnp.exp(p['d_a_log'].astype(f32)) * jax.nn.softplus(col(OFF_DA, H_D).astype(f32)
                                                              + p['d_dt_bias'])
    o, s_new = gated_delta_chunked(q, k, v, g, beta, delta.astype(f32))
    z = col(OFF_DZ, W_D).astype(f32).reshape(bsz, l, H_D, DV)
    o = o * lax.rsqrt(jnp.mean(o * o, axis=-1, keepdims=True) + RMS_EPS) * p['d_norm_g'] * jax.nn.silu(z)
    y_d = o.reshape(bsz, l, W_D).astype(x.dtype)

    ys = jnp.stack([y_a, y_b, y_c, y_d], axis=2)
    yb = jnp.einsum('blnw,nwd->blnd', ys, p['w_branch'])
    gates = jax.nn.sigmoid(col(OFF_G, N_BRANCH * D_MODEL).reshape(bsz, l, N_BRANCH, D_MODEL)
                           + p['b_gate'])
    mixed = jnp.einsum('blnd,blnd->bld', gates, yb)
    out = mixed @ p['w_out']
    x = layer_norm(ALPHA * x + out, p['ln_g'], p['ln_b'])
    return x, (new_conv_a, new_conv_b, new_lru, new_conv_c, new_conv_d, s_new.astype(x.dtype))


def run_trunk(x, states, ln_in_g, ln_in_b, params):
    x = layer_norm(x, ln_in_g, ln_in_b)
    new = [[] for _ in states]
    for i in range(DEPTH):
        p = {name: arr[i] for name, arr in params.items()}
        x, layer_states = mixer_layer(x, tuple(s[i] for s in states), p)
        for acc, s in zip(new, layer_states):
            acc.append(s)
    return x, tuple(jnp.stack(acc) for acc in new)


def setup_inputs(seed: int = 0) -> dict:
    key = jax.random.key(seed)
    ks = jax.random.split(key, 40)

    def nrm(i, shape, scale):
        return scale * jax.random.normal(ks[i], shape, f32)

    a0 = jax.random.uniform(ks[30], (DEPTH, W_B), f32, minval=0.9, maxval=0.999)
    r = a0 ** (1.0 / LRU_C)
    b_lambda = jnp.log(r) - jnp.log1p(-r)
    d_a_log = jnp.log(jax.random.uniform(ks[31], (DEPTH, H_D), f32, minval=1.0, maxval=16.0))
    dt = jnp.exp(jax.random.uniform(ks[32], (DEPTH, H_D), f32,
                                    minval=math.log(1e-3), maxval=math.log(1e-1)))
    d_dt_bias = dt + jnp.log(-jnp.expm1(-dt))
    return {
        'x_prompt': nrm(0, (BATCH, SEQ, D_MODEL), 1.0),
        'x_sample': nrm(1, (DEC_BATCH, DEC_SEQ, D_MODEL), 1.0),
        'state_conv_a': nrm(2, (DEPTH, DEC_BATCH, K_A - 1, W_A), 1.0),
        'state_conv_b': nrm(3, (DEPTH, DEC_BATCH, K_B - 1, W_B), 1.0),
        'state_lru': nrm(4, (DEPTH, DEC_BATCH, W_B), 0.5),
        'state_conv_c': nrm(5, (DEPTH, DEC_BATCH, K_C - 1, W_C), 1.0),
        'state_conv_d': nrm(6, (DEPTH, DEC_BATCH, K_D - 1, 2 * W_QK + W_D), 1.0),
        'state_delta': nrm(7, (DEPTH, DEC_BATCH, H_D, DK, DV), 0.1),
        'ln_in_g': 1.0 + nrm(8, (D_MODEL,), 0.02),
        'ln_in_b': nrm(9, (D_MODEL,), 0.02),
        'w_in': nrm(10, (DEPTH, D_MODEL, N_IN), D_MODEL ** -0.5),
        'b_gate': nrm(11, (DEPTH, N_BRANCH, D_MODEL), 0.1),
        'a_conv_w': nrm(12, (DEPTH, K_A, W_A), K_A ** -0.5),
        'b_conv_w': nrm(13, (DEPTH, K_B, W_B), K_B ** -0.5),
        'b_conv_b': nrm(14, (DEPTH, W_B), 0.02),
        'b_wx': nrm(15, (DEPTH, H_B, BW_B, BW_B), BW_B ** -0.5),
        'b_bx': nrm(16, (DEPTH, H_B, BW_B), 0.02),
        'b_wa': nrm(17, (DEPTH, H_B, BW_B, BW_B), BW_B ** -0.5),
        'b_ba': nrm(18, (DEPTH, H_B, BW_B), 0.02),
        'b_lambda': b_lambda,
        'c_conv_w': nrm(19, (DEPTH, K_C, W_C), K_C ** -0.5),
        'c_conv_b': nrm(20, (DEPTH, W_C), 0.02),
        'c_ln_g': 1.0 + nrm(21, (DEPTH, W_C), 0.02),
        'c_ln_b': nrm(22, (DEPTH, W_C), 0.02),
        'd_conv_w': nrm(23, (DEPTH, K_D, 2 * W_QK + W_D), K_D ** -0.5),
        'd_a_log': d_a_log,
        'd_dt_bias': d_dt_bias,
        'd_norm_g': 1.0 + nrm(24, (DEPTH, DV), 0.02),
        'w_branch': nrm(25, (DEPTH, N_BRANCH, W_BR, D_MODEL), BETA_INIT * W_BR ** -0.5),
        'w_out': nrm(26, (DEPTH, D_MODEL, D_MODEL), BETA_INIT * D_MODEL ** -0.5),
        'ln_g': 1.0 + nrm(27, (DEPTH, D_MODEL), 0.02),
        'ln_b': nrm(28, (DEPTH, D_MODEL), 0.02),
    }


def reference(x_prompt, x_sample, state_conv_a, state_conv_b, state_lru, state_conv_c,
              state_conv_d, state_delta, ln_in_g, ln_in_b, w_in, b_gate, a_conv_w, b_conv_w,
              b_conv_b, b_wx, b_bx, b_wa, b_ba, b_lambda, c_conv_w, c_conv_b, c_ln_g, c_ln_b,
              d_conv_w, d_a_log, d_dt_bias, d_norm_g, w_branch, w_out, ln_g, ln_b):
    params = dict(w_in=w_in, b_gate=b_gate, a_conv_w=a_conv_w, b_conv_w=b_conv_w,
                  b_conv_b=b_conv_b, b_wx=b_wx, b_bx=b_bx, b_wa=b_wa, b_ba=b_ba,
                  b_lambda=b_lambda, c_conv_w=c_conv_w, c_conv_b=c_conv_b, c_ln_g=c_ln_g,
                  c_ln_b=c_ln_b, d_conv_w=d_conv_w, d_a_log=d_a_log, d_dt_bias=d_dt_bias,
                  d_norm_g=d_norm_g, w_branch=w_branch, w_out=w_out, ln_g=ln_g, ln_b=ln_b)
    bp = x_prompt.shape[0]
    dt = x_prompt.dtype

    def zeros(*shape):
        return jnp.zeros((DEPTH, bp) + shape, dt)

    prompt_states = (zeros(K_A - 1, W_A), zeros(K_B - 1, W_B), zeros(W_B),
                     zeros(K_C - 1, W_C), zeros(K_D - 1, 2 * W_QK + W_D), zeros(H_D, DK, DV))
    y_prompt, p_st = run_trunk(x_prompt, prompt_states, ln_in_g, ln_in_b, params)
    sample_states = (state_conv_a, state_conv_b, state_lru, state_conv_c, state_conv_d, state_delta)
    y_sample, s_st = run_trunk(x_sample, sample_states, ln_in_g, ln_in_b, params)

    p_conv_a, p_conv_b, p_lru, p_conv_c, p_conv_d, p_delta = p_st
    s_conv_a, s_conv_b, s_lru, s_conv_c, s_conv_d, s_delta = s_st
    return (y_prompt, y_sample, p_conv_a, p_conv_b, p_lru, p_conv_c, p_conv_d, p_delta,
            s_conv_a, s_conv_b, s_lru, s_conv_c, s_conv_d, s_delta)
```

```python
import functools

import jax
import jax.numpy as jnp
from jax import lax
from jax.experimental import pallas as pl
from jax.experimental.pallas import tpu as pltpu

f32 = jnp.float32
bf16 = jnp.bfloat16

LN_EPS = 1e-5
RMS_EPS = 1e-6
L2_EPS = 1e-6
LRU_C = 8.0
GDN_CHUNK = 64
NB = 8
LANES = 128
ROW_SUB = 256
VMEM_LIMIT = 56 * 1024 * 1024


def _silu(x):
    return x * jax.nn.sigmoid(x)


def _softplus(x):
    return jnp.maximum(x, 0.0) + jnp.log1p(jnp.exp(-jnp.abs(x)))


def _layer_norm(x, g, b):
    mu = jnp.mean(x, axis=-1, keepdims=True)
    xc = x - mu
    var = jnp.mean(xc * xc, axis=-1, keepdims=True)
    return xc * lax.rsqrt(var + LN_EPS) * g + b


def _dot(a, b):
    return jnp.dot(a, b, preferred_element_type=f32)


def _full(shape):
    n = len(shape)
    return pl.BlockSpec(shape, lambda *_: (0,) * n)


def _params(n_axes):
    return pltpu.CompilerParams(dimension_semantics=("arbitrary",) * n_axes,
                                vmem_limit_bytes=VMEM_LIMIT)


def _causal_conv(cbuf, cw, r0, rs, nb):
    acc = None
    for k in range(cw.shape[0]):
        t = cbuf[pl.ds(pl.multiple_of(r0 + k * nb, NB), rs), :] * cw[k:k + 1, :]
        acc = t if acc is None else acc + t
    return acc


def _ln_in_kernel(x_ref, g_ref, b_ref, xo_ref, xb_ref):
    y = _layer_norm(x_ref[...], g_ref[...], b_ref[...])
    xo_ref[...] = y
    xb_ref[...] = y.astype(bf16)


def _ln_in(x, g, b, rows):
    t, d = x.shape
    return pl.pallas_call(
        _ln_in_kernel,
        grid=(t // rows,),
        in_specs=[pl.BlockSpec((rows, d), lambda i: (i, 0)), _full((1, d)), _full((1, d))],
        out_specs=[pl.BlockSpec((rows, d), lambda i: (i, 0)), pl.BlockSpec((rows, d), lambda i: (i, 0))],
        out_shape=[jax.ShapeDtypeStruct((t, d), f32), jax.ShapeDtypeStruct((t, d), bf16)],
        compiler_params=_params(1),
        name="ln_in",
    )(x, g, b)


def _branch_a_kernel(xb_ref, w_ref, st_ref, cw_ref, y_ref, sto_ref, cbuf, *, rs):
    t_idx = pl.program_id(1)
    rows, w = y_ref.shape
    halo = st_ref.shape[0]

    @pl.when(t_idx == 0)
    def _():
        cbuf[0:halo, :] = st_ref[...]

    cw = cw_ref[...]

    def sub(j, carry):
        r0 = pl.multiple_of(j * rs, rs)
        p = _dot(xb_ref[pl.ds(r0, rs), :], w_ref[...])
        a_b, a_c, a_x, a_z = (p[:, n * w:(n + 1) * w] for n in range(4))
        cbuf[pl.ds(pl.multiple_of(r0 + halo, NB), rs), :] = a_c * a_x
        conv = _causal_conv(cbuf, cw, r0, rs, NB)
        y_ref[pl.ds(r0, rs), :] = (a_b * conv * _silu(a_z)).astype(bf16)
        return carry

    lax.fori_loop(0, rows // rs, sub, 0)
    new = cbuf[rows:rows + halo, :]
    cbuf[0:halo, :] = new
    sto_ref[...] = new


def _branch_b_kernel(xb_ref, w_ref, st_ref, h0_ref, cw_ref, cb_ref, wxa_ref, bx_ref, ba_ref,
                     lam_ref, y_ref, sto_ref, ho_ref, cbuf, h_scr, *, rs):
    t_idx = pl.program_id(1)
    rows, w = y_ref.shape
    halo = st_ref.shape[0]
    n_blk, bw, _ = wxa_ref.shape

    @pl.when(t_idx == 0)
    def _():
        cbuf[0:halo, :] = st_ref[...]
        h_scr[...] = h0_ref[...]

    cw = cw_ref[...]
    neg_c_sp = -LRU_C * _softplus(-lam_ref[...])

    def sub(j, carry):
        r0 = pl.multiple_of(j * rs, rs)
        p = _dot(xb_ref[pl.ds(r0, rs), :], w_ref[...])
        cbuf[pl.ds(pl.multiple_of(r0 + halo, NB), rs), :] = p[:, 0:w]
        b_z = p[:, w:2 * w]
        xc = _causal_conv(cbuf, cw, r0, rs, NB) + cb_ref[...]
        xc16 = xc.astype(bf16)
        pre = [_dot(xc16[:, h * bw:(h + 1) * bw], wxa_ref[h]) for h in range(n_blk)]
        gate_x = jax.nn.sigmoid(jnp.concatenate([q[:, 0:bw] for q in pre], axis=1) + bx_ref[...])
        gate_a = jax.nn.sigmoid(jnp.concatenate([q[:, bw:2 * bw] for q in pre], axis=1) + ba_ref[...])
        a = jnp.exp(gate_a * neg_c_sp)
        bterm = jnp.sqrt(1.0 - a * a) * gate_x * xc
        h = h_scr[...]
        hs = []
        for s in range(rs // NB):
            h = a[s * NB:(s + 1) * NB, :] * h + bterm[s * NB:(s + 1) * NB, :]
            hs.append(h)
        h_scr[...] = h
        y_ref[pl.ds(r0, rs), :] = (jnp.concatenate(hs, axis=0) * _silu(b_z)).astype(bf16)
        return carry

    lax.fori_loop(0, rows // rs, sub, 0)
    new = cbuf[rows:rows + halo, :]
    cbuf[0:halo, :] = new
    sto_ref[...] = new
    ho_ref[...] = h_scr[...]


def _branch_c_kernel(xb_ref, w_ref, st_ref, cw_ref, cb_ref, g_ref, b_ref, y_ref, sto_ref, cbuf, *, rs):
    t_idx = pl.program_id(1)
    rows, w = y_ref.shape
    halo = st_ref.shape[0]

    @pl.when(t_idx == 0)
    def _():
        cbuf[0:halo, :] = st_ref[...]

    cw = cw_ref[...]

    def sub(j, carry):
        r0 = pl.multiple_of(j * rs, rs)
        p = _dot(xb_ref[pl.ds(r0, rs), :], w_ref[...])
        cbuf[pl.ds(pl.multiple_of(r0 + halo, NB), rs), :] = p[:, 0:w] * jax.nn.sigmoid(p[:, w:2 * w])
        c_z = p[:, 2 * w:3 * w]
        conv = _causal_conv(cbuf, cw, r0, rs, NB) + cb_ref[...]
        y = _silu(_layer_norm(conv, g_ref[...], b_ref[...])) * _silu(c_z)
        y_ref[pl.ds(r0, rs), :] = y.astype(bf16)
        return carry

    lax.fori_loop(0, rows // rs, sub, 0)
    new = cbuf[rows:rows + halo, :]
    cbuf[0:halo, :] = new
    sto_ref[...] = new


def _branch_d1_kernel(xb_ref, w_ref, wab_ref, st_ref, cw_ref, dpar_ref, q_ref, k_ref, v_ref, z_ref,
                      gb_ref, sto_ref, cbuf, *, rs, n_heads):
    t_idx = pl.program_id(1)
    rows, w = q_ref.shape
    halo = st_ref.shape[0]
    dk = w // n_heads

    @pl.when(t_idx == 0)
    def _():
        cbuf[0:halo, :] = st_ref[...]

    cw = cw_ref[...]
    neg_a = -jnp.exp(dpar_ref[0:1, :])
    dt_bias = dpar_ref[1:2, :]
    lane = lax.broadcasted_iota(jnp.int32, (rs, LANES), 1)

    def l2n(x, scale):
        parts = []
        for h in range(n_heads):
            xh = x[:, h * dk:(h + 1) * dk]
            ss = jnp.sum(xh * xh, axis=-1, keepdims=True)
            parts.append(xh * (lax.rsqrt(ss + L2_EPS) * scale))
        return jnp.concatenate(parts, axis=1)

    def sub(j, carry):
        r0 = pl.multiple_of(j * rs, rs)
        xb = xb_ref[pl.ds(r0, rs), :]
        p = _dot(xb, w_ref[...])
        pab = _dot(xb, wab_ref[...])
        cbuf[pl.ds(pl.multiple_of(r0 + halo, NB), rs), :] = p[:, 0:3 * w]
        qkv = _silu(_causal_conv(cbuf, cw, r0, rs, NB))
        q_ref[pl.ds(r0, rs), :] = l2n(qkv[:, 0:w], dk ** -0.5).astype(bf16)
        k_ref[pl.ds(r0, rs), :] = l2n(qkv[:, w:2 * w], 1.0).astype(bf16)
        v_ref[pl.ds(r0, rs), :] = qkv[:, 2 * w:3 * w].astype(bf16)
        z_ref[pl.ds(r0, rs), :] = _silu(p[:, 3 * w:4 * w])
        g = neg_a * _softplus(pab + dt_bias)
        beta = jax.nn.sigmoid(pab)
        gb_ref[pl.ds(r0, rs), :] = jnp.where(lane < n_heads, g, beta)
        return carry

    lax.fori_loop(0, rows // rs, sub, 0)
    new = cbuf[rows:rows + halo, :]
    cbuf[0:halo, :] = new
    sto_ref[...] = new


def _unit_lower_inverse(a, row, col):
    c = a.shape[0]

    def mm(x, y):
        return _dot(x.astype(bf16), y.astype(bf16))

    def blk(i, size):
        return lax.shift_right_logical(i, size.bit_length() - 1)

    eye = (row == col).astype(f32)
    a0 = jnp.where(blk(row, 8) == blk(col, 8), a, 0.0)
    a2 = mm(a0, a0)
    a4 = mm(a2, a2)
    x = eye - a0
    x = x + mm(x, a2)
    x = x + mm(x, a4)
    s = 8
    while s < c:
        join = (blk(row, 2 * s) == blk(col, 2 * s)) & (blk(row, s) != blk(col, s))
        low = jnp.where(join, a, 0.0)
        x = x - mm(mm(x, low), x)
        s *= 2
    return x


def _branch_d2_kernel(q_ref, k_ref, v_ref, z_ref, gb_ref, s0_ref, ng_ref, y_ref, so_ref, s_scr, *, n_heads):
    c_idx = pl.program_id(2)
    c, w = q_ref.shape
    dk = w // n_heads

    @pl.when(c_idx == 0)
    def _():
        s_scr[...] = s0_ref[...]

    gb = gb_ref[...]
    row = lax.broadcasted_iota(jnp.int32, (c, c), 0)
    col = lax.broadcasted_iota(jnp.int32, (c, c), 1)
    incl = row >= col
    strict = row > col
    ltri = incl.astype(bf16)
    g_hi = gb.astype(bf16)
    r1 = gb - g_hi.astype(f32)
    g_mid = r1.astype(bf16)
    g_lo = (r1 - g_mid.astype(f32)).astype(bf16)
    gc = _dot(ltri, g_hi) + _dot(ltri, g_mid) + _dot(ltri, g_lo)
    pad = jnp.zeros((LANES - c, LANES), f32)
    gc_t = jnp.concatenate([gc, pad], axis=0).T
    gb_t = jnp.concatenate([gb, pad], axis=0).T
    g_last = gc[c - 1:c, :]
    e_gc = jnp.exp(gc)
    e_rest = jnp.exp(g_last - gc)
    e_last = jnp.exp(g_last)
    e_gc_t = jnp.exp(gc_t)
    ng = ng_ref[...]

    for h in range(n_heads):
        sl = slice(h * dk, (h + 1) * dk)
        q = q_ref[:, sl]
        k = k_ref[:, sl]
        v = v_ref[:, sl]
        kk = lax.dot_general(k, k, (((1,), (1,)), ((), ())), preferred_element_type=f32)
        qk = lax.dot_general(q, k, (((1,), (1,)), ((), ())), preferred_element_type=f32)
        diff = gc[:, h:h + 1] - gc_t[h:h + 1, 0:c]
        decay = jnp.exp(jnp.where(incl, diff, -jnp.inf))
        a_mat = jnp.where(strict, kk * decay * gb[:, n_heads + h:n_heads + h + 1], 0.0)
        t_inv = _unit_lower_inverse(a_mat, row, col)
        t_b = t_inv * gb_t[n_heads + h:n_heads + h + 1, 0:c]
        t_be = t_b * e_gc_t[h:h + 1, 0:c]
        u_base = _dot(t_b.astype(bf16), v)
        w_mat = _dot(t_be.astype(bf16), k)
        s = s_scr[h]
        s16 = s.astype(bf16)
        u = u_base - _dot(w_mat.astype(bf16), s16)
        u16 = u.astype(bf16)
        o = _dot(q, s16) * e_gc[:, h:h + 1] + _dot((qk * decay).astype(bf16), u16)
        u_rest = (u * e_rest[:, h:h + 1]).astype(bf16)
        s_new = s * e_last[:, h:h + 1] + lax.dot_general(
            k, u_rest, (((0,), (0,)), ((), ())), preferred_element_type=f32)
        s_scr[h] = s_new
        ms = jnp.mean(o * o, axis=-1, keepdims=True)
        y_ref[:, sl] = (o * lax.rsqrt(ms + RMS_EPS) * ng * z_ref[:, sl]).astype(bf16)

    so_ref[...] = s_scr[...]


def _merge_kernel(x_ref, xb_ref, ya_ref, yb_ref, yc_ref, yd_ref, wg_ref, bg_ref, wbr_ref, wo_ref,
                  g_ref, b_ref, xo_ref, xbo_ref, *, rs, alpha):
    rows, d = x_ref.shape
    y_refs = (ya_ref, yb_ref, yc_ref, yd_ref)

    def sub(j, carry):
        r0 = pl.multiple_of(j * rs, rs)
        rsl = pl.ds(r0, rs)
        gates = jax.nn.sigmoid(_dot(xb_ref[rsl, :], wg_ref[...]) + bg_ref[...])
        mixed = None
        for n, yr in enumerate(y_refs):
            t = gates[:, n * d:(n + 1) * d] * _dot(yr[rsl, :], wbr_ref[n])
            mixed = t if mixed is None else mixed + t
        out = _dot(mixed.astype(bf16), wo_ref[...])
        xn = _layer_norm(alpha * x_ref[rsl, :] + out, g_ref[...], b_ref[...])
        xo_ref[rsl, :] = xn
        xbo_ref[rsl, :] = xn.astype(bf16)
        return carry

    lax.fori_loop(0, rows // rs, sub, 0)


def _tile_spec(rows, width, n_t):
    return pl.BlockSpec((rows, width), lambda g, t: (g * n_t + t, 0))


def _group_spec(shape):
    n = len(shape)
    return pl.BlockSpec((None,) + tuple(shape), lambda g, t: (g,) + (0,) * n)


def _const_spec(shape):
    n = len(shape)
    return pl.BlockSpec(tuple(shape), lambda g, t: (0,) * n, pipeline_mode=pl.Buffered(1))


def _layer(x, xb, st, p, *, n_groups, seq_len, tt, alpha):
    conv_a, conv_b, lru, conv_c, conv_d, delta = st
    t_rows, d = x.shape
    w = p["a_conv_w"].shape[1]
    n_t = seq_len // tt
    rows = tt * NB
    rs = min(ROW_SUB, rows)
    grid = (n_groups, n_t)
    tile_in = _tile_spec(rows, d, n_t)
    tile_w = _tile_spec(rows, w, n_t)
    act16 = jax.ShapeDtypeStruct((t_rows, w), bf16)

    def st_shape(a):
        return jax.ShapeDtypeStruct(a.shape, f32)

    def cbuf(a):
        return pltpu.VMEM((a.shape[1] + rows, a.shape[2]), f32)

    y_a, new_a = pl.pallas_call(
        functools.partial(_branch_a_kernel, rs=rs),
        grid=grid,
        in_specs=[tile_in, _const_spec(p["w_a"].shape), _group_spec(conv_a.shape[1:]),
                  _const_spec(p["a_conv_w"].shape)],
        out_specs=[tile_w, _group_spec(conv_a.shape[1:])],
        out_shape=[act16, st_shape(conv_a)],
        scratch_shapes=[cbuf(conv_a)],
        compiler_params=_params(2),
        name="branch_a",
    )(xb, p["w_a"], conv_a, p["a_conv_w"])

    y_b, new_b, new_lru = pl.pallas_call(
        functools.partial(_branch_b_kernel, rs=rs),
        grid=grid,
        in_specs=[tile_in, _const_spec(p["w_b"].shape), _group_spec(conv_b.shape[1:]),
                  _group_spec(lru.shape[1:]), _const_spec(p["b_conv_w"].shape),
                  _const_spec(p["b_conv_b"].shape), _const_spec(p["b_wxa"].shape),
                  _const_spec(p["b_bx"].shape), _const_spec(p["b_ba"].shape),
                  _const_spec(p["b_lambda"].shape)],
        out_specs=[tile_w, _group_spec(conv_b.shape[1:]), _group_spec(lru.shape[1:])],
        out_shape=[act16, st_shape(conv_b), st_shape(lru)],
        scratch_shapes=[cbuf(conv_b), pltpu.VMEM(lru.shape[1:], f32)],
        compiler_params=_params(2),
        name="branch_b",
    )(xb, p["w_b"], conv_b, lru, p["b_conv_w"], p["b_conv_b"], p["b_wxa"], p["b_bx"], p["b_ba"],
      p["b_lambda"])

    y_c, new_c = pl.pallas_call(
        functools.partial(_branch_c_kernel, rs=rs),
        grid=grid,
        in_specs=[tile_in, _const_spec(p["w_c"].shape), _group_spec(conv_c.shape[1:]),
                  _const_spec(p["c_conv_w"].shape), _const_spec(p["c_conv_b"].shape),
                  _const_spec(p["c_ln_g"].shape), _const_spec(p["c_ln_b"].shape)],
        out_specs=[tile_w, _group_spec(conv_c.shape[1:])],
        out_shape=[act16, st_shape(conv_c)],
        scratch_shapes=[cbuf(conv_c)],
        compiler_params=_params(2),
        name="branch_c",
    )(xb, p["w_c"], conv_c, p["c_conv_w"], p["c_conv_b"], p["c_ln_g"], p["c_ln_b"])

    n_heads = delta.shape[1]
    q, k, v, z, gb, new_d = pl.pallas_call(
        functools.partial(_branch_d1_kernel, rs=rs, n_heads=n_heads),
        grid=grid,
        in_specs=[tile_in, _const_spec(p["w_d"].shape), _const_spec(p["w_dab"].shape),
                  _group_spec(conv_d.shape[1:]), _const_spec(p["d_conv_w"].shape),
                  _const_spec(p["d_par"].shape)],
        out_specs=[tile_w, tile_w, tile_w, tile_w, _tile_spec(rows, LANES, n_t),
                   _group_spec(conv_d.shape[1:])],
        out_shape=[act16, act16, act16, jax.ShapeDtypeStruct((t_rows, w), f32),
                   jax.ShapeDtypeStruct((t_rows, LANES), f32), st_shape(conv_d)],
        scratch_shapes=[cbuf(conv_d)],
        compiler_params=_params(2),
        name="branch_d_proj",
    )(xb, p["w_d"], p["w_dab"], conv_d, p["d_conv_w"], p["d_par"])

    chunk = min(GDN_CHUNK, seq_len)
    n_c = seq_len // chunk

    def seq_view(a):
        return a.reshape(n_groups, seq_len, NB * a.shape[1])

    def seq_spec(width):
        return pl.BlockSpec((None, chunk, width), lambda g, n, c: (g, c, n))

    dk, dv = delta.shape[2], delta.shape[3]
    state_spec = pl.BlockSpec((None, n_heads, dk, dv), lambda g, n, c: (g * NB + n, 0, 0, 0))
    y_d, new_delta = pl.pallas_call(
        functools.partial(_branch_d2_kernel, n_heads=n_heads),
        grid=(n_groups, NB, n_c),
        in_specs=[seq_spec(w), seq_spec(w), seq_spec(w), seq_spec(w), seq_spec(LANES), state_spec,
                  pl.BlockSpec(p["d_norm_g"].shape, lambda g, n, c: (0, 0))],
        out_specs=[seq_spec(w), state_spec],
        out_shape=[jax.ShapeDtypeStruct((n_groups, seq_len, NB * w), bf16), st_shape(delta)],
        scratch_shapes=[pltpu.VMEM((n_heads, dk, dv), f32)],
        compiler_params=_params(3),
        name="branch_d_delta",
    )(seq_view(q), seq_view(k), seq_view(v), seq_view(z), seq_view(gb), delta, p["d_norm_g"])
    y_d = y_d.reshape(t_rows, w)

    x_new, xb_new = pl.pallas_call(
        functools.partial(_merge_kernel, rs=rs, alpha=alpha),
        grid=grid,
        in_specs=[tile_in, tile_in, tile_w, tile_w, tile_w, tile_w, _const_spec(p["w_g"].shape),
                  _const_spec(p["b_gate"].shape), _const_spec(p["w_branch"].shape),
                  _const_spec(p["w_out"].shape), _const_spec(p["ln_g"].shape),
                  _const_spec(p["ln_b"].shape)],
        out_specs=[tile_in, tile_in],
        out_shape=[jax.ShapeDtypeStruct((t_rows, d), f32), jax.ShapeDtypeStruct((t_rows, d), bf16)],
        compiler_params=_params(2),
        name="merge",
    )(x, xb, y_a, y_b, y_c, y_d, p["w_g"], p["b_gate"], p["w_branch"], p["w_out"], p["ln_g"], p["ln_b"])

    return x_new, xb_new, (new_a, new_b, new_lru, new_c, new_d, new_delta)


def _rows_from_seq(x):
    b, l, d = x.shape
    return x.reshape(b // NB, NB, l, d).transpose(0, 2, 1, 3).reshape(b * l, d)


def _seq_from_rows(x, b, l):
    d = x.shape[-1]
    return x.reshape(b // NB, l, NB, d).transpose(0, 2, 1, 3).reshape(b, l, d)


def _conv_state_to_rows(s):
    dep, b, k1, c = s.shape
    return s.reshape(dep, b // NB, NB, k1, c).transpose(0, 1, 3, 2, 4).reshape(dep, b // NB, k1 * NB, c)


def _conv_state_from_rows(s, b):
    dep, g, r, c = s.shape
    k1 = r // NB
    return s.reshape(dep, g, k1, NB, c).transpose(0, 1, 3, 2, 4).reshape(dep, b, k1, c)


def _run_trunk(x, states, ln_in_g, ln_in_b, params, alpha):
    b, l, d = x.shape
    assert b % NB == 0
    n_groups = b // NB
    tt = min(l, 64)
    assert l % tt == 0 and l % min(GDN_CHUNK, l) == 0
    conv_a, conv_b, lru, conv_c, conv_d, delta = states
    st = (_conv_state_to_rows(conv_a), _conv_state_to_rows(conv_b),
          lru.reshape(lru.shape[0], n_groups, NB, lru.shape[-1]),
          _conv_state_to_rows(conv_c), _conv_state_to_rows(conv_d), delta)
    rows = _rows_from_seq(x)
    x0, xb0 = _ln_in(rows, ln_in_g, ln_in_b, tt * NB)

    def step(carry, xs):
        xc, xbc = carry
        p, s = xs
        xn, xbn, new = _layer(xc, xbc, s, p, n_groups=n_groups, seq_len=l, tt=tt, alpha=alpha)
        return (xn, xbn), new

    (xf, _), new = lax.scan(step, (x0, xb0), (params, st))
    na, nb_, nl, nc, nd, ndelta = new
    y = _seq_from_rows(xf, b, l)
    return y, (_conv_state_from_rows(na, b), _conv_state_from_rows(nb_, b),
               nl.reshape(nl.shape[0], b, nl.shape[-1]), _conv_state_from_rows(nc, b),
               _conv_state_from_rows(nd, b), ndelta)


def kernel(x_prompt, x_sample, state_conv_a, state_conv_b, state_lru, state_conv_c, state_conv_d, state_delta, ln_in_g, ln_in_b, w_in, b_gate, a_conv_w, b_conv_w, b_conv_b, b_wx, b_bx, b_wa, b_ba, b_lambda, c_conv_w, c_conv_b, c_ln_g, c_ln_b, d_conv_w, d_a_log, d_dt_bias, d_norm_g, w_branch, w_out, ln_g, ln_b):
    depth, d, _ = w_in.shape
    w_a, w_b, w_c = a_conv_w.shape[2], b_conv_w.shape[2], c_conv_w.shape[2]
    w_qkv = d_conv_w.shape[2]
    n_heads, dv = d_a_log.shape[1], d_norm_g.shape[1]
    w_d = n_heads * dv
    assert n_heads <= LANES // 2
    off_b = 4 * w_a
    off_c = off_b + 2 * w_b
    off_d = off_c + 3 * w_c
    off_da = off_d + w_qkv + w_d
    off_g = off_da + 2 * n_heads
    alpha = (2 * depth) ** 0.25

    def row(a):
        return a.reshape(depth, 1, -1)

    def cols(lo, hi):
        return w_in[:, :, lo:hi].astype(bf16)

    d_par = jnp.zeros((depth, 8, LANES), f32)
    d_par = d_par.at[:, 0, :n_heads].set(d_a_log).at[:, 1, :n_heads].set(d_dt_bias)
    params = dict(
        w_a=cols(0, off_b), w_b=cols(off_b, off_c), w_c=cols(off_c, off_d),
        w_d=cols(off_d, off_da),
        w_dab=jnp.pad(w_in[:, :, off_da:off_g], ((0, 0), (0, 0), (0, LANES - 2 * n_heads))).astype(bf16),
        w_g=cols(off_g, off_g + b_gate.shape[1] * d),
        b_gate=row(b_gate), a_conv_w=a_conv_w, b_conv_w=b_conv_w, b_conv_b=row(b_conv_b),
        b_wxa=jnp.concatenate([b_wx, b_wa], axis=-1).astype(bf16),
        b_bx=row(b_bx), b_ba=row(b_ba), b_lambda=row(b_lambda),
        c_conv_w=c_conv_w, c_conv_b=row(c_conv_b), c_ln_g=row(c_ln_g), c_ln_b=row(c_ln_b),
        d_conv_w=d_conv_w, d_par=d_par, d_norm_g=row(d_norm_g),
        w_branch=w_branch.astype(bf16), w_out=w_out.astype(bf16), ln_g=row(ln_g), ln_b=row(ln_b),
    )
    g_in, b_in = ln_in_g.reshape(1, d), ln_in_b.reshape(1, d)

    bp = x_prompt.shape[0]

    def zeros(s):
        return jnp.zeros((depth, bp) + s.shape[2:], s.dtype)

    prompt_states = tuple(zeros(s) for s in (state_conv_a, state_conv_b, state_lru, state_conv_c,
                                              state_conv_d, state_delta))
    y_prompt, p_st = _run_trunk(x_prompt, prompt_states, g_in, b_in, params, alpha)
    sample_states = (state_conv_a, state_conv_b, state_lru, state_conv_c, state_conv_d, state_delta)
    y_sample, s_st = _run_trunk(x_sample, sample_states, g_in, b_in, params, alpha)
    return (y_prompt, y_sample) + tuple(p_st) + tuple(s_st)
```

```python
import functools

import jax
import jax.numpy as jnp
from jax import lax
from jax.experimental import pallas as pl
from jax.experimental.pallas import tpu as pltpu

f32 = jnp.float32
bf16 = jnp.bfloat16

LN_EPS = 1e-5
RMS_EPS = 1e-6
L2_EPS = 1e-6
LRU_C = 8.0
GDN_CHUNK = 64
NB = 8
LANES = 128
ROW_SUB = 256
VMEM_LIMIT = 56 * 1024 * 1024


def _silu(x):
    return x * jax.nn.sigmoid(x)


def _softplus(x):
    return jnp.maximum(x, 0.0) + jnp.log1p(jnp.exp(-jnp.abs(x)))


def _layer_norm(x, g, b):
    mu = jnp.mean(x, axis=-1, keepdims=True)
    xc = x - mu
    var = jnp.mean(xc * xc, axis=-1, keepdims=True)
    return xc * lax.rsqrt(var + LN_EPS) * g + b


def _dot(a, b):
    return jnp.dot(a, b, preferred_element_type=f32)


def _full(shape):
    n = len(shape)
    return pl.BlockSpec(shape, lambda *_: (0,) * n)


def _params(n_axes):
    return pltpu.CompilerParams(dimension_semantics=("arbitrary",) * n_axes,
                                vmem_limit_bytes=VMEM_LIMIT)


def _causal_conv(cbuf, cw, r0, rs, nb):
    acc = None
    for k in range(cw.shape[0]):
        t = cbuf[pl.ds(pl.multiple_of(r0 + k * nb, NB), rs), :] * cw[k:k + 1, :]
        acc = t if acc is None else acc + t
    return acc


def _ln_in_kernel(x_ref, g_ref, b_ref, xo_ref, xb_ref):
    y = _layer_norm(x_ref[...], g_ref[...], b_ref[...])
    xo_ref[...] = y
    xb_ref[...] = y.astype(bf16)


def _ln_in(x, g, b, rows):
    t, d = x.shape
    return pl.pallas_call(
        _ln_in_kernel,
        grid=(t // rows,),
        in_specs=[pl.BlockSpec((rows, d), lambda i: (i, 0)), _full((1, d)), _full((1, d))],
        out_specs=[pl.BlockSpec((rows, d), lambda i: (i, 0)), pl.BlockSpec((rows, d), lambda i: (i, 0))],
        out_shape=[jax.ShapeDtypeStruct((t, d), f32), jax.ShapeDtypeStruct((t, d), bf16)],
        compiler_params=_params(1),
        name="ln_in",
    )(x, g, b)


def _branch_a_kernel(xb_ref, w_ref, st_ref, cw_ref, y_ref, sto_ref, cbuf, *, rs):
    t_idx = pl.program_id(1)
    rows, w = y_ref.shape
    halo = st_ref.shape[0]

    @pl.when(t_idx == 0)
    def _():
        cbuf[0:halo, :] = st_ref[...]

    cw = cw_ref[...]

    def sub(j, carry):
        r0 = pl.multiple_of(j * rs, rs)
        p = _dot(xb_ref[pl.ds(r0, rs), :], w_ref[...])
        a_b, a_c, a_x, a_z = (p[:, n * w:(n + 1) * w] for n in range(4))
        cbuf[pl.ds(pl.multiple_of(r0 + halo, NB), rs), :] = a_c * a_x
        conv = _causal_conv(cbuf, cw, r0, rs, NB)
        y_ref[pl.ds(r0, rs), :] = (a_b * conv * _silu(a_z)).astype(bf16)
        return carry

    lax.fori_loop(0, rows // rs, sub, 0)
    new = cbuf[rows:rows + halo, :]
    cbuf[0:halo, :] = new
    sto_ref[...] = new


def _branch_b_kernel(xb_ref, w_ref, st_ref, h0_ref, cw_ref, cb_ref, wxa_ref, bx_ref, ba_ref,
                     lam_ref, y_ref, sto_ref, ho_ref, cbuf, h_scr, *, rs):
    t_idx = pl.program_id(1)
    rows, w = y_ref.shape
    halo = st_ref.shape[0]
    n_blk, bw, _ = wxa_ref.shape

    @pl.when(t_idx == 0)
    def _():
        cbuf[0:halo, :] = st_ref[...]
        h_scr[...] = h0_ref[...]

    cw = cw_ref[...]
    neg_c_sp = -LRU_C * _softplus(-lam_ref[...])

    def sub(j, carry):
        r0 = pl.multiple_of(j * rs, rs)
        p = _dot(xb_ref[pl.ds(r0, rs), :], w_ref[...])
        cbuf[pl.ds(pl.multiple_of(r0 + halo, NB), rs), :] = p[:, 0:w]
        b_z = p[:, w:2 * w]
        xc = _causal_conv(cbuf, cw, r0, rs, NB) + cb_ref[...]
        xc16 = xc.astype(bf16)
        pre = [_dot(xc16[:, h * bw:(h + 1) * bw], wxa_ref[h]) for h in range(n_blk)]
        gate_x = jax.nn.sigmoid(jnp.concatenate([q[:, 0:bw] for q in pre], axis=1) + bx_ref[...])
        gate_a = jax.nn.sigmoid(jnp.concatenate([q[:, bw:2 * bw] for q in pre], axis=1) + ba_ref[...])
        a = jnp.exp(gate_a * neg_c_sp)
        bterm = jnp.sqrt(1.0 - a * a) * gate_x * xc
        h = h_scr[...]
        hs = []
        for s in range(rs // NB):
            h = a[s * NB:(s + 1) * NB, :] * h + bterm[s * NB:(s + 1) * NB, :]
            hs.append(h)
        h_scr[...] = h
        y_ref[pl.ds(r0, rs), :] = (jnp.concatenate(hs, axis=0) * _silu(b_z)).astype(bf16)
        return carry

    lax.fori_loop(0, rows // rs, sub, 0)
    new = cbuf[rows:rows + halo, :]
    cbuf[0:halo, :] = new
    sto_ref[...] = new
    ho_ref[...] = h_scr[...]


def _branch_c_kernel(xb_ref, w_ref, st_ref, cw_ref, cb_ref, g_ref, b_ref, y_ref, sto_ref, cbuf, *, rs):
    t_idx = pl.program_id(1)
    rows, w = y_ref.shape
    halo = st_ref.shape[0]

    @pl.when(t_idx == 0)
    def _():
        cbuf[0:halo, :] = st_ref[...]

    cw = cw_ref[...]

    def sub(j, carry):
        r0 = pl.multiple_of(j * rs, rs)
        p = _dot(xb_ref[pl.ds(r0, rs), :], w_ref[...])
        cbuf[pl.ds(pl.multiple_of(r0 + halo, NB), rs), :] = p[:, 0:w] * jax.nn.sigmoid(p[:, w:2 * w])
        c_z = p[:, 2 * w:3 * w]
        conv = _causal_conv(cbuf, cw, r0, rs, NB) + cb_ref[...]
        y = _silu(_layer_norm(conv, g_ref[...], b_ref[...])) * _silu(c_z)
        y_ref[pl.ds(r0, rs), :] = y.astype(bf16)
        return carry

    lax.fori_loop(0, rows // rs, sub, 0)
    new = cbuf[rows:rows + halo, :]
    cbuf[0:halo, :] = new
    sto_ref[...] = new


def _branch_d1_kernel(xb_ref, w_ref, wab_ref, st_ref, cw_ref, dpar_ref, q_ref, k_ref, v_ref, z_ref,
                      gb_ref, sto_ref, cbuf, *, rs, n_heads):
    t_idx = pl.program_id(1)
    _, rows, dk = q_ref.shape
    w = n_heads * dk
    halo = st_ref.shape[0]

    @pl.when(t_idx == 0)
    def _():
        cbuf[0:halo, :] = st_ref[...]

    cw = cw_ref[...]
    neg_a = -jnp.exp(dpar_ref[0:1, :])
    dt_bias = dpar_ref[1:2, :]
    lane = lax.broadcasted_iota(jnp.int32, (rs, LANES), 1)

    def l2n(xh, scale):
        ss = jnp.sum(xh * xh, axis=-1, keepdims=True)
        return xh * (lax.rsqrt(ss + L2_EPS) * scale)

    def sub(j, carry):
        r0 = pl.multiple_of(j * rs, rs)
        rsl = pl.ds(r0, rs)
        xb = xb_ref[rsl, :]
        p = _dot(xb, w_ref[...])
        pab = _dot(xb, wab_ref[...])
        cbuf[pl.ds(pl.multiple_of(r0 + halo, NB), rs), :] = p[:, 0:3 * w]
        qkv = _silu(_causal_conv(cbuf, cw, r0, rs, NB))
        for h in range(n_heads):
            q_ref[h, rsl, :] = l2n(qkv[:, h * dk:(h + 1) * dk], dk ** -0.5)
            k_ref[h, rsl, :] = l2n(qkv[:, w + h * dk:w + (h + 1) * dk], 1.0)
            v_ref[h, rsl, :] = qkv[:, 2 * w + h * dk:2 * w + (h + 1) * dk]
            z_ref[h, rsl, :] = _silu(p[:, 3 * w + h * dk:3 * w + (h + 1) * dk])
        g = neg_a * _softplus(pab + dt_bias)
        beta = jax.nn.sigmoid(pab)
        gb_ref[pl.ds(r0, rs), :] = jnp.where(lane < n_heads, g, beta)
        return carry

    lax.fori_loop(0, rows // rs, sub, 0)
    new = cbuf[rows:rows + halo, :]
    cbuf[0:halo, :] = new
    sto_ref[...] = new


def _unit_lower_inverse(a, row, col):
    c = a[0].shape[0]

    def mm(xs, ys):
        return [_dot(x.astype(bf16), y.astype(bf16)) for x, y in zip(xs, ys)]

    def blk(i, size):
        return lax.shift_right_logical(i, size.bit_length() - 1)

    eye = (row == col).astype(f32)
    base = blk(row, 8) == blk(col, 8)
    a0 = [jnp.where(base, m, 0.0) for m in a]
    a2 = mm(a0, a0)
    a4 = mm(a2, a2)
    x = [eye - m for m in a0]
    x = [p + q for p, q in zip(x, mm(x, a2))]
    x = [p + q for p, q in zip(x, mm(x, a4))]
    s = 8
    while s < c:
        join = (blk(row, 2 * s) == blk(col, 2 * s)) & (blk(row, s) != blk(col, s))
        low = [jnp.where(join, m, 0.0) for m in a]
        x = [p - q for p, q in zip(x, mm(mm(x, low), x))]
        s *= 2
    return x


def _branch_d2_kernel(q_ref, k_ref, v_ref, z_ref, gb_ref, s0_ref, ng_ref, y_ref, so_ref, s_scr, y_scr,
                      *, n_heads):
    c_idx = pl.program_id(1)
    _, rows, dk = q_ref.shape
    c = rows // NB

    @pl.when(c_idx == 0)
    def _():
        s_scr[...] = s0_ref[...]

    row = lax.broadcasted_iota(jnp.int32, (c, c), 0)
    col = lax.broadcasted_iota(jnp.int32, (c, c), 1)
    incl = row >= col
    strict = row > col
    ltri = incl.astype(bf16)
    pad = jnp.zeros((LANES - c, LANES), f32)
    ng = ng_ref[...]
    hs = range(n_heads)
    nt = (((1,), (1,)), ((), ()))
    tn = (((0,), (0,)), ((), ()))

    def one_sequence(n, carry):
        seq = pl.ds(n, c, stride=NB)
        gb = gb_ref[seq, :]
        g_hi = gb.astype(bf16)
        r1 = gb - g_hi.astype(f32)
        g_mid = r1.astype(bf16)
        g_lo = (r1 - g_mid.astype(f32)).astype(bf16)
        gc = _dot(ltri, g_hi) + _dot(ltri, g_mid) + _dot(ltri, g_lo)
        gc_t = jnp.concatenate([gc, pad], axis=0).T
        gb_t = jnp.concatenate([gb, pad], axis=0).T
        g_last = gc[c - 1:c, :]
        e_gc = jnp.exp(gc)
        e_rest = jnp.exp(g_last - gc)
        e_last = jnp.exp(g_last)
        e_gc_t = jnp.exp(gc_t)

        q = [q_ref[h, seq, :].astype(bf16) for h in hs]
        k = [k_ref[h, seq, :].astype(bf16) for h in hs]
        v = [v_ref[h, seq, :].astype(bf16) for h in hs]
        kk = [lax.dot_general(k[h], k[h], nt, preferred_element_type=f32) for h in hs]
        qk = [lax.dot_general(q[h], k[h], nt, preferred_element_type=f32) for h in hs]
        decay = [jnp.exp(jnp.where(incl, gc[:, h:h + 1] - gc_t[h:h + 1, 0:c], -jnp.inf)) for h in hs]
        a_mat = [jnp.where(strict, kk[h] * decay[h] * gb[:, n_heads + h:n_heads + h + 1], 0.0)
                 for h in hs]
        t_inv = _unit_lower_inverse(a_mat, row, col)
        t_b = [t_inv[h] * gb_t[n_heads + h:n_heads + h + 1, 0:c] for h in hs]
        t_be = [t_b[h] * e_gc_t[h:h + 1, 0:c] for h in hs]
        u_base = [_dot(t_b[h].astype(bf16), v[h]) for h in hs]
        w_mat = [_dot(t_be[h].astype(bf16), k[h]) for h in hs]
        s = [s_scr[n, h] for h in hs]
        s16 = [m.astype(bf16) for m in s]
        u = [u_base[h] - _dot(w_mat[h].astype(bf16), s16[h]) for h in hs]
        qs = [_dot(q[h], s16[h]) for h in hs]
        pu = [_dot((qk[h] * decay[h]).astype(bf16), u[h].astype(bf16)) for h in hs]
        ku = [lax.dot_general(k[h], (u[h] * e_rest[:, h:h + 1]).astype(bf16), tn,
                              preferred_element_type=f32) for h in hs]
        for h in hs:
            s_scr[n, h] = s[h] * e_last[:, h:h + 1] + ku[h]
            o = qs[h] * e_gc[:, h:h + 1] + pu[h]
            ms = jnp.mean(o * o, axis=-1, keepdims=True)
            y_scr[h, seq, :] = o * lax.rsqrt(ms + RMS_EPS) * ng * z_ref[h, seq, :]
        return carry

    lax.fori_loop(0, NB, one_sequence, 0)
    for h in hs:
        y_ref[:, h * dk:(h + 1) * dk] = y_scr[h].astype(bf16)
    so_ref[...] = s_scr[...]


def _merge_kernel(x_ref, xb_ref, ya_ref, yb_ref, yc_ref, yd_ref, wg_ref, bg_ref, wbr_ref, wo_ref,
                  g_ref, b_ref, xo_ref, xbo_ref, *, rs, alpha):
    rows, d = x_ref.shape
    y_refs = (ya_ref, yb_ref, yc_ref, yd_ref)

    def sub(j, carry):
        r0 = pl.multiple_of(j * rs, rs)
        rsl = pl.ds(r0, rs)
        gates = jax.nn.sigmoid(_dot(xb_ref[rsl, :], wg_ref[...]) + bg_ref[...])
        mixed = None
        for n, yr in enumerate(y_refs):
            t = gates[:, n * d:(n + 1) * d] * _dot(yr[rsl, :], wbr_ref[n])
            mixed = t if mixed is None else mixed + t
        out = _dot(mixed.astype(bf16), wo_ref[...])
        xn = _layer_norm(alpha * x_ref[rsl, :] + out, g_ref[...], b_ref[...])
        xo_ref[rsl, :] = xn
        xbo_ref[rsl, :] = xn.astype(bf16)
        return carry

    lax.fori_loop(0, rows // rs, sub, 0)


def _tile_spec(rows, width, n_t):
    return pl.BlockSpec((rows, width), lambda g, t: (g * n_t + t, 0))


def _group_spec(shape):
    n = len(shape)
    return pl.BlockSpec((None,) + tuple(shape), lambda g, t: (g,) + (0,) * n)


def _const_spec(shape):
    n = len(shape)
    return pl.BlockSpec(tuple(shape), lambda g, t: (0,) * n, pipeline_mode=pl.Buffered(1))


def _layer(x, xb, st, p, *, n_groups, seq_len, tt, alpha):
    conv_a, conv_b, lru, conv_c, conv_d, delta = st
    t_rows, d = x.shape
    w = p["a_conv_w"].shape[1]
    n_t = seq_len // tt
    rows = tt * NB
    rs = min(ROW_SUB, rows)
    grid = (n_groups, n_t)
    tile_in = _tile_spec(rows, d, n_t)
    tile_w = _tile_spec(rows, w, n_t)
    act16 = jax.ShapeDtypeStruct((t_rows, w), bf16)

    def st_shape(a):
        return jax.ShapeDtypeStruct(a.shape, f32)

    def cbuf(a):
        return pltpu.VMEM((a.shape[1] + rows, a.shape[2]), f32)

    y_a, new_a = pl.pallas_call(
        functools.partial(_branch_a_kernel, rs=rs),
        grid=grid,
        in_specs=[tile_in, _const_spec(p["w_a"].shape), _group_spec(conv_a.shape[1:]),
                  _const_spec(p["a_conv_w"].shape)],
        out_specs=[tile_w, _group_spec(conv_a.shape[1:])],
        out_shape=[act16, st_shape(conv_a)],
        scratch_shapes=[cbuf(conv_a)],
        compiler_params=_params(2),
        name="branch_a",
    )(xb, p["w_a"], conv_a, p["a_conv_w"])

    y_b, new_b, new_lru = pl.pallas_call(
        functools.partial(_branch_b_kernel, rs=rs),
        grid=grid,
        in_specs=[tile_in, _const_spec(p["w_b"].shape), _group_spec(conv_b.shape[1:]),
                  _group_spec(lru.shape[1:]), _const_spec(p["b_conv_w"].shape),
                  _const_spec(p["b_conv_b"].shape), _const_spec(p["b_wxa"].shape),
                  _const_spec(p["b_bx"].shape), _const_spec(p["b_ba"].shape),
                  _const_spec(p["b_lambda"].shape)],
        out_specs=[tile_w, _group_spec(conv_b.shape[1:]), _group_spec(lru.shape[1:])],
        out_shape=[act16, st_shape(conv_b), st_shape(lru)],
        scratch_shapes=[cbuf(conv_b), pltpu.VMEM(lru.shape[1:], f32)],
        compiler_params=_params(2),
        name="branch_b",
    )(xb, p["w_b"], conv_b, lru, p["b_conv_w"], p["b_conv_b"], p["b_wxa"], p["b_bx"], p["b_ba"],
      p["b_lambda"])

    y_c, new_c = pl.pallas_call(
        functools.partial(_branch_c_kernel, rs=rs),
        grid=grid,
        in_specs=[tile_in, _const_spec(p["w_c"].shape), _group_spec(conv_c.shape[1:]),
                  _const_spec(p["c_conv_w"].shape), _const_spec(p["c_conv_b"].shape),
                  _const_spec(p["c_ln_g"].shape), _const_spec(p["c_ln_b"].shape)],
        out_specs=[tile_w, _group_spec(conv_c.shape[1:])],
        out_shape=[act16, st_shape(conv_c)],
        scratch_shapes=[cbuf(conv_c)],
        compiler_params=_params(2),
        name="branch_c",
    )(xb, p["w_c"], conv_c, p["c_conv_w"], p["c_conv_b"], p["c_ln_g"], p["c_ln_b"])

    n_heads, dk, dv = delta.shape[1:]
    assert dk == LANES and dv == LANES
    head_rows = jax.ShapeDtypeStruct((n_heads, t_rows, dk), f32)
    q, k, v, z, gb, new_d = pl.pallas_call(
        functools.partial(_branch_d1_kernel, rs=rs, n_heads=n_heads),
        grid=grid,
        in_specs=[tile_in, _const_spec(p["w_d"].shape), _const_spec(p["w_dab"].shape),
                  _group_spec(conv_d.shape[1:]), _const_spec(p["d_conv_w"].shape),
                  _const_spec(p["d_par"].shape)],
        out_specs=[pl.BlockSpec((n_heads, rows, dk), lambda g, t: (0, g * n_t + t, 0))] * 4
        + [_tile_spec(rows, LANES, n_t), _group_spec(conv_d.shape[1:])],
        out_shape=[head_rows] * 4 + [jax.ShapeDtypeStruct((t_rows, LANES), f32), st_shape(conv_d)],
        scratch_shapes=[cbuf(conv_d)],
        compiler_params=_params(2),
        name="branch_d_proj",
    )(xb, p["w_d"], p["w_dab"], conv_d, p["d_conv_w"], p["d_par"])

    chunk = min(GDN_CHUNK, seq_len)
    n_c = seq_len // chunk
    c_rows = chunk * NB
    head_spec = pl.BlockSpec((n_heads, c_rows, dk), lambda g, c: (0, g * n_c + c, 0))
    state_spec = pl.BlockSpec((NB, n_heads, dk, dv), lambda g, c: (g, 0, 0, 0))
    y_d, new_delta = pl.pallas_call(
        functools.partial(_branch_d2_kernel, n_heads=n_heads),
        grid=(n_groups, n_c),
        in_specs=[head_spec, head_spec, head_spec, head_spec, _tile_spec(c_rows, LANES, n_c),
                  state_spec, _const_spec(p["d_norm_g"].shape)],
        out_specs=[_tile_spec(c_rows, w, n_c), state_spec],
        out_shape=[act16, st_shape(delta)],
        scratch_shapes=[pltpu.VMEM((NB, n_heads, dk, dv), f32), pltpu.VMEM((n_heads, c_rows, dk), f32)],
        compiler_params=_params(2),
        name="branch_d_delta",
    )(q, k, v, z, gb, delta, p["d_norm_g"])

    x_new, xb_new = pl.pallas_call(
        functools.partial(_merge_kernel, rs=rs, alpha=alpha),
        grid=grid,
        in_specs=[tile_in, tile_in, tile_w, tile_w, tile_w, tile_w, _const_spec(p["w_g"].shape),
                  _const_spec(p["b_gate"].shape), _const_spec(p["w_branch"].shape),
                  _const_spec(p["w_out"].shape), _const_spec(p["ln_g"].shape),
                  _const_spec(p["ln_b"].shape)],
        out_specs=[tile_in, tile_in],
        out_shape=[jax.ShapeDtypeStruct((t_rows, d), f32), jax.ShapeDtypeStruct((t_rows, d), bf16)],
        compiler_params=_params(2),
        name="merge",
    )(x, xb, y_a, y_b, y_c, y_d, p["w_g"], p["b_gate"], p["w_branch"], p["w_out"], p["ln_g"], p["ln_b"])

    return x_new, xb_new, (new_a, new_b, new_lru, new_c, new_d, new_delta)


def _rows_from_seq(x):
    b, l, d = x.shape
    return x.reshape(b // NB, NB, l, d).transpose(0, 2, 1, 3).reshape(b * l, d)


def _seq_from_rows(x, b, l):
    d = x.shape[-1]
    return x.reshape(b // NB, l, NB, d).transpose(0, 2, 1, 3).reshape(b, l, d)


def _conv_state_to_rows(s):
    dep, b, k1, c = s.shape
    return s.reshape(dep, b // NB, NB, k1, c).transpose(0, 1, 3, 2, 4).reshape(dep, b // NB, k1 * NB, c)


def _conv_state_from_rows(s, b):
    dep, g, r, c = s.shape
    k1 = r // NB
    return s.reshape(dep, g, k1, NB, c).transpose(0, 1, 3, 2, 4).reshape(dep, b, k1, c)


def _run_trunk(x, states, ln_in_g, ln_in_b, params, alpha):
    b, l, d = x.shape
    assert b % NB == 0
    n_groups = b // NB
    tt = min(l, 64)
    assert l % tt == 0 and l % min(GDN_CHUNK, l) == 0
    conv_a, conv_b, lru, conv_c, conv_d, delta = states
    st = (_conv_state_to_rows(conv_a), _conv_state_to_rows(conv_b),
          lru.reshape(lru.shape[0], n_groups, NB, lru.shape[-1]),
          _conv_state_to_rows(conv_c), _conv_state_to_rows(conv_d), delta)
    rows = _rows_from_seq(x)
    x0, xb0 = _ln_in(rows, ln_in_g, ln_in_b, tt * NB)

    def step(carry, xs):
        xc, xbc = carry
        p, s = xs
        xn, xbn, new = _layer(xc, xbc, s, p, n_groups=n_groups, seq_len=l, tt=tt, alpha=alpha)
        return (xn, xbn), new

    (xf, _), new = lax.scan(step, (x0, xb0), (params, st))
    na, nb_, nl, nc, nd, ndelta = new
    y = _seq_from_rows(xf, b, l)
    return y, (_conv_state_from_rows(na, b), _conv_state_from_rows(nb_, b),
               nl.reshape(nl.shape[0], b, nl.shape[-1]), _conv_state_from_rows(nc, b),
               _conv_state_from_rows(nd, b), ndelta)


def kernel(x_prompt, x_sample, state_conv_a, state_conv_b, state_lru, state_conv_c, state_conv_d, state_delta, ln_in_g, ln_in_b, w_in, b_gate, a_conv_w, b_conv_w, b_conv_b, b_wx, b_bx, b_wa, b_ba, b_lambda, c_conv_w, c_conv_b, c_ln_g, c_ln_b, d_conv_w, d_a_log, d_dt_bias, d_norm_g, w_branch, w_out, ln_g, ln_b):
    depth, d, _ = w_in.shape
    w_a, w_b, w_c = a_conv_w.shape[2], b_conv_w.shape[2], c_conv_w.shape[2]
    w_qkv = d_conv_w.shape[2]
    n_heads, dv = d_a_log.shape[1], d_norm_g.shape[1]
    w_d = n_heads * dv
    assert n_heads <= LANES // 2
    off_b = 4 * w_a
    off_c = off_b + 2 * w_b
    off_d = off_c + 3 * w_c
    off_da = off_d + w_qkv + w_d
    off_g = off_da + 2 * n_heads
    alpha = (2 * depth) ** 0.25

    def row(a):
        return a.reshape(depth, 1, -1)

    def cols(lo, hi):
        return w_in[:, :, lo:hi].astype(bf16)

    d_par = jnp.zeros((depth, 8, LANES), f32)
    d_par = d_par.at[:, 0, :n_heads].set(d_a_log).at[:, 1, :n_heads].set(d_dt_bias)
    params = dict(
        w_a=cols(0, off_b), w_b=cols(off_b, off_c), w_c=cols(off_c, off_d),
        w_d=cols(off_d, off_da),
        w_dab=jnp.pad(w_in[:, :, off_da:off_g], ((0, 0), (0, 0), (0, LANES - 2 * n_heads))).astype(bf16),
        w_g=cols(off_g, off_g + b_gate.shape[1] * d),
        b_gate=row(b_gate), a_conv_w=a_conv_w, b_conv_w=b_conv_w, b_conv_b=row(b_conv_b),
        b_wxa=jnp.concatenate([b_wx, b_wa], axis=-1).astype(bf16),
        b_bx=row(b_bx), b_ba=row(b_ba), b_lambda=row(b_lambda),
        c_conv_w=c_conv_w, c_conv_b=row(c_conv_b), c_ln_g=row(c_ln_g), c_ln_b=row(c_ln_b),
        d_conv_w=d_conv_w, d_par=d_par, d_norm_g=row(d_norm_g),
        w_branch=w_branch.astype(bf16), w_out=w_out.astype(bf16), ln_g=row(ln_g), ln_b=row(ln_b),
    )
    g_in, b_in = ln_in_g.reshape(1, d), ln_in_b.reshape(1, d)

    bp = x_prompt.shape[0]

    def zeros(s):
        return jnp.zeros((depth, bp) + s.shape[2:], s.dtype)

    prompt_states = tuple(zeros(s) for s in (state_conv_a, state_conv_b, state_lru, state_conv_c,
                                              state_conv_d, state_delta))
    y_prompt, p_st = _run_trunk(x_prompt, prompt_states, g_in, b_in, params, alpha)
    sample_states = (state_conv_a, state_conv_b, state_lru, state_conv_c, state_conv_d, state_delta)
    y_sample, s_st = _run_trunk(x_sample, sample_states, g_in, b_in, params, alpha)
    return (y_prompt, y_sample) + tuple(p_st) + tuple(s_st)
```

```python
import functools

import jax
import jax.numpy as jnp
from jax import lax
from jax.experimental import pallas as pl
from jax.experimental.pallas import tpu as pltpu

f32 = jnp.float32
bf16 = jnp.bfloat16

LN_EPS = 1e-5
RMS_EPS = 1e-6
L2_EPS = 1e-6
LRU_C = 8.0
GDN_CHUNK = 64
NB = 8
LANES = 128
PACK = 2 * LANES
DELTA_SEQS = 4
TILE_FRAMES = 32
VMEM_LIMIT = 56 * 1024 * 1024


def _silu(x):
    return x * jax.nn.sigmoid(x)


def _softplus(x):
    return jnp.maximum(x, 0.0) + jnp.log1p(jnp.exp(-jnp.abs(x)))


def _layer_norm(x, g, b):
    mu = jnp.mean(x, axis=-1, keepdims=True)
    xc = x - mu
    var = jnp.mean(xc * xc, axis=-1, keepdims=True)
    return xc * lax.rsqrt(var + LN_EPS) * g + b


def _dot(a, b):
    return jnp.dot(a, b, preferred_element_type=f32)


def _params(n_axes):
    return pltpu.CompilerParams(dimension_semantics=("arbitrary",) * n_axes,
                                vmem_limit_bytes=VMEM_LIMIT)


def _halo_init(cbuf, st_ref, t_idx):
    @pl.when(t_idx == 0)
    def _():
        cbuf[0:st_ref.shape[0], :] = st_ref[...]


def _causal_conv(cbuf, u, cw, halo):
    rows = u.shape[0]
    cbuf[halo:halo + rows, :] = u
    acc = None
    for k in range(cw.shape[0]):
        t = cbuf[k * NB:k * NB + rows, :] * cw[k:k + 1, :]
        acc = t if acc is None else acc + t
    return acc


def _halo_roll(cbuf, sto_ref, rows):
    halo = sto_ref.shape[0]
    new = cbuf[rows:rows + halo, :]
    cbuf[0:halo, :] = new
    sto_ref[...] = new


def _ln_in_kernel(x_ref, g_ref, b_ref, xo_ref, xb_ref):
    y = _layer_norm(x_ref[...], g_ref[...], b_ref[...])
    xo_ref[...] = y
    xb_ref[...] = y.astype(bf16)


def _ln_in(x, g, b, rows):
    t, d = x.shape
    tile = pl.BlockSpec((rows, d), lambda i: (i, 0))
    vec = pl.BlockSpec((1, d), lambda i: (0, 0))
    return pl.pallas_call(
        _ln_in_kernel,
        grid=(t // rows,),
        in_specs=[tile, vec, vec],
        out_specs=[tile, tile],
        out_shape=[jax.ShapeDtypeStruct((t, d), f32), jax.ShapeDtypeStruct((t, d), bf16)],
        compiler_params=_params(1),
        name="ln_in",
    )(x, g, b)


def _mix_in_kernel(xb_ref, wa_ref, wb_ref, wd_ref, wdab_ref, sta_ref, stb_ref, h0_ref, std_ref,
                   acw_ref, bcw_ref, bcb_ref, wxa_ref, bx_ref, ba_ref, lam_ref, dcw_ref, dpar_ref,
                   ya_ref, yb_ref, q_ref, k_ref, v_ref, z_ref, gb_ref,
                   stoa_ref, stob_ref, ho_ref, stod_ref,
                   cba, cbb, cbd, h_scr):
    t_idx = pl.program_id(1)
    rows, w = ya_ref.shape
    n_heads, _, dk = q_ref.shape
    n_blk, bw, _ = wxa_ref.shape
    _halo_init(cba, sta_ref, t_idx)
    _halo_init(cbb, stb_ref, t_idx)
    _halo_init(cbd, std_ref, t_idx)

    @pl.when(t_idx == 0)
    def _():
        h_scr[...] = h0_ref[...]

    xb = xb_ref[...]

    pa = _dot(xb, wa_ref[...])
    a_b, a_c, a_x, a_z = (pa[:, n * w:(n + 1) * w] for n in range(4))
    conv = _causal_conv(cba, a_c * a_x, acw_ref[...], sta_ref.shape[0])
    ya_ref[...] = (a_b * conv * _silu(a_z)).astype(bf16)
    _halo_roll(cba, stoa_ref, rows)

    pb = _dot(xb, wb_ref[...])
    xc = _causal_conv(cbb, pb[:, 0:w], bcw_ref[...], stb_ref.shape[0]) + bcb_ref[...]
    _halo_roll(cbb, stob_ref, rows)
    xc16 = xc.astype(bf16)
    pre = [_dot(xc16[:, h * bw:(h + 1) * bw], wxa_ref[h]) for h in range(n_blk)]
    gate_x = jax.nn.sigmoid(jnp.concatenate([g[:, 0:bw] for g in pre], axis=1) + bx_ref[...])
    gate_a = jax.nn.sigmoid(jnp.concatenate([g[:, bw:2 * bw] for g in pre], axis=1) + ba_ref[...])
    a = jnp.exp(gate_a * (-LRU_C * _softplus(-lam_ref[...])))
    bterm = jnp.sqrt(1.0 - a * a) * gate_x * xc
    h = h_scr[...]
    hs = []
    for s in range(rows // NB):
        h = a[s * NB:(s + 1) * NB, :] * h + bterm[s * NB:(s + 1) * NB, :]
        hs.append(h)
    h_scr[...] = h
    ho_ref[...] = h
    yb_ref[...] = (jnp.concatenate(hs, axis=0) * _silu(pb[:, w:2 * w])).astype(bf16)

    pd = _dot(xb, wd_ref[...])
    pab = _dot(xb, wdab_ref[...])
    qkv = _silu(_causal_conv(cbd, pd[:, 0:3 * w], dcw_ref[...], std_ref.shape[0]))
    _halo_roll(cbd, stod_ref, rows)

    def l2n(xh, scale):
        ss = jnp.sum(xh * xh, axis=-1, keepdims=True)
        return xh * (lax.rsqrt(ss + L2_EPS) * scale)

    for hd in range(n_heads):
        q_ref[hd] = l2n(qkv[:, hd * dk:(hd + 1) * dk], dk ** -0.5)
        k_ref[hd] = l2n(qkv[:, w + hd * dk:w + (hd + 1) * dk], 1.0)
        v_ref[hd] = qkv[:, 2 * w + hd * dk:2 * w + (hd + 1) * dk]
        z_ref[hd] = _silu(pd[:, 3 * w + hd * dk:3 * w + (hd + 1) * dk])
    g = -jnp.exp(dpar_ref[0:1, :]) * _softplus(pab + dpar_ref[1:2, :])
    lane = lax.broadcasted_iota(jnp.int32, pab.shape, 1)
    gb_ref[...] = jnp.where(lane < n_heads, g, jax.nn.sigmoid(pab))


def _split_bf16(x, parts):
    out = []
    for _ in range(parts):
        t = x.astype(bf16)
        out.append(t)
        x = x - t.astype(f32)
    return out


def _delta_kernel(q_ref, k_ref, v_ref, z_ref, gb_ref, s0_ref, ng_ref, y_ref, so_ref, s_scr, y_scr,
                  *, n_heads, seqs_per_iter):
    c_idx = pl.program_id(1)
    _, rows, dk = q_ref.shape
    c = rows // NB
    hg = PACK // c
    n_grp = n_heads // hg
    n_pair = n_heads // 2
    ppg = hg // 2
    spi = seqs_per_iter
    assert c * hg == PACK and n_grp * hg == n_heads and 2 * ppg == hg and 2 * dk == PACK
    sh_c = c.bit_length() - 1

    @pl.when(c_idx == 0)
    def _():
        s_scr[...] = jnp.zeros(s_scr.shape, f32)
        for n in range(NB):
            for h in range(n_heads):
                o = (h % 2) * dk
                s_scr[n, h // 2, o:o + dk, o:o + dk] = s0_ref[n, h]

    def iota(shape, axis):
        return lax.broadcasted_iota(jnp.int32, shape, axis)

    def blk(i, size):
        return lax.shift_right_logical(i, size.bit_length() - 1)

    row = iota((c, PACK), 0)
    lane = iota((c, PACK), 1)
    jcol = lane & (c - 1)
    incl = row >= jcol
    strict = row > jcol
    eye_f = (row == jcol).astype(f32)
    head_sel = [(blk(lane, c) == hh).astype(bf16) for hh in range(hg)]
    base = blk(row, 8) == blk(jcol, 8)
    joins = []
    s = 8
    while s < c:
        joins.append((blk(row, 2 * s) == blk(jcol, 2 * s)) & (blk(row, s) != blk(jcol, s)))
        s *= 2
    ltri = (iota((c, c), 0) >= iota((c, c), 1)).astype(bf16)
    s_diag = blk(iota((PACK, PACK), 0), dk) == blk(iota((PACK, PACK), 1), dk)
    lane128 = iota((c, LANES), 1)
    is_g = lane128 < n_heads
    heads_per_tile = LANES // c
    zero16 = jnp.zeros((c, dk), bf16)
    ng = ng_ref[...]
    nt = (((1,), (1,)), ((), ()))
    tn = (((0,), (0,)), ((), ()))
    chains = [(i, g) for i in range(spi) for g in range(n_grp)]
    pairs = [(i, p) for i in range(spi) for p in range(n_pair)]

    def bd_heads(blocks):
        return jnp.concatenate(
            [jnp.concatenate([blocks[i] if i == j else zero16 for j in range(hg)], axis=1)
             for i in range(hg)], axis=0)

    def mmp(xs, ys):
        out = []
        for x, y in zip(xs, ys):
            y16 = y.astype(bf16)
            out.append(_dot(x.astype(bf16), jnp.concatenate([y16 * m for m in head_sel], axis=0)))
        return out

    def iteration(it, carry):
        seqs = [it * spi + i for i in range(spi)]
        sls = [pl.ds(n, c, stride=NB) for n in seqs]
        sq = range(spi)

        gbs = [gb_ref[sl, :] for sl in sls]
        cs = [_dot(ltri, jnp.concatenate(_split_bf16(jnp.where(is_g, gb, 0.0), 3), axis=1)) for gb in gbs]
        gc = [t[:, 0:LANES] + t[:, LANES:2 * LANES] + t[:, 2 * LANES:3 * LANES] for t in cs]

        def per_head_lanes(a, first):
            return jnp.concatenate([jnp.broadcast_to(a[:, first + h:first + h + 1], (c, dk))
                                    for h in range(n_heads)], axis=1)

        def packed_cols(full, g):
            tiles = []
            for t in range(PACK // LANES):
                first = g * hg + t * heads_per_tile
                val = full[:, first * dk:(first + 1) * dk]
                for m in range(1, heads_per_tile):
                    val = jnp.where(lane128 >= m * c, full[:, (first + m) * dk:(first + m + 1) * dk], val)
                tiles.append(val)
            return jnp.concatenate(tiles, axis=1)

        gc_full = [per_head_lanes(t, 0) for t in gc]
        beta_full = [per_head_lanes(t, n_heads) for t in gbs]
        gcol = [packed_cols(gc_full[i], g) for i, g in chains]
        grow = [jnp.sum(t * eye_f, axis=0, keepdims=True) for t in gcol]
        decay = [jnp.exp(jnp.where(incl, gcol[j] - grow[j], -jnp.inf)) for j in range(len(chains))]
        g_last = [t[c - 1:c, :] for t in gc_full]
        e_gc = [jnp.exp(t) for t in gc_full]
        e_rest = [jnp.exp(g_last[i] - gc_full[i]) for i in sq]
        e_last = [jnp.exp(t) for t in g_last]

        def heads16(a):
            return [a[:, h * dk:(h + 1) * dk].astype(bf16) for h in range(n_heads)]

        def load(ref, sl):
            return jnp.concatenate([ref[h, sl, :] for h in range(n_heads)], axis=1)

        qh = [heads16(load(q_ref, sl)) for sl in sls]
        k32 = [load(k_ref, sl) for sl in sls]
        kh = [heads16(t) for t in k32]
        kb32 = [k32[i] * beta_full[i] for i in sq]
        kbh = [heads16(t) for t in kb32]
        kbeh = [heads16(kb32[i] * e_gc[i]) for i in sq]
        vbh = [heads16(load(v_ref, sls[i]) * beta_full[i]) for i in sq]
        kbd = [bd_heads(kh[i][g * hg:(g + 1) * hg]) for i, g in chains]
        rhs_uw = [jnp.concatenate([bd_heads(vbh[i][g * hg:(g + 1) * hg]),
                                   bd_heads(kbeh[i][g * hg:(g + 1) * hg])], axis=1) for i, g in chains]
        kq = [jnp.concatenate([jnp.concatenate(kbh[i][g * hg:(g + 1) * hg], axis=1),
                               jnp.concatenate(qh[i][g * hg:(g + 1) * hg], axis=1)], axis=0)
              for i, g in chains]
        kkqk = [lax.dot_general(kq[j], kbd[j], nt, preferred_element_type=f32) for j in range(len(chains))]
        a_mat = [jnp.where(strict, kkqk[j][0:c] * decay[j], 0.0) for j in range(len(chains))]
        p_mat = [(kkqk[j][c:2 * c] * decay[j]).astype(bf16) for j in range(len(chains))]

        a0 = [jnp.where(base, m, 0.0) for m in a_mat]
        a2 = mmp(a0, a0)
        a4 = mmp(a2, a2)
        x = [eye_f - m for m in a0]
        x = [p + q for p, q in zip(x, mmp(x, a2))]
        x = [p + q for p, q in zip(x, mmp(x, a4))]
        for join in joins:
            low = [jnp.where(join, m, 0.0) for m in a_mat]
            x = [p - q for p, q in zip(x, mmp(mmp(x, low), x))]

        uw = [_dot(x[j].astype(bf16), rhs_uw[j]) for j in range(len(chains))]
        u_base = [t[:, 0:hg * dk] for t in uw]
        w_mat = [t[:, hg * dk:2 * hg * dk] for t in uw]

        def of_pair(arrs, i, p):
            return arrs[i * n_grp + p // ppg][:, (p % ppg) * PACK:(p % ppg + 1) * PACK]

        s_old = [s_scr[seqs[i], p] for i, p in pairs]
        s16 = [t.astype(bf16) for t in s_old]
        k_p = [jnp.concatenate(kh[i][2 * p:2 * p + 2], axis=1) for i, p in pairs]
        wq = [_dot(jnp.concatenate([of_pair(w_mat, i, p).astype(bf16),
                                    jnp.concatenate(qh[i][2 * p:2 * p + 2], axis=1)], axis=0), s16[m])
              for m, (i, p) in enumerate(pairs)]
        u = [of_pair(u_base, i, p) - wq[m][0:c] for m, (i, p) in enumerate(pairs)]
        u16 = [t.astype(bf16) for t in u]
        ubd = [bd_heads([u16[i * n_pair + (g * hg + hh) // 2][:, (hh % 2) * dk:(hh % 2 + 1) * dk]
                         for hh in range(hg)]) for i, g in chains]
        pu = [_dot(p_mat[j], ubd[j]) for j in range(len(chains))]
        ktu = [lax.dot_general(k_p[m], (u[m] * e_rest[i][:, p * PACK:(p + 1) * PACK]).astype(bf16), tn,
                               preferred_element_type=f32) for m, (i, p) in enumerate(pairs)]
        for m, (i, p) in enumerate(pairs):
            s_scr[seqs[i], p] = (s_old[m] * e_last[i][:, p * PACK:(p + 1) * PACK]
                                 + jnp.where(s_diag, ktu[m], 0.0))
        for j, (i, g) in enumerate(chains):
            qs = jnp.concatenate([wq[i * n_pair + g * ppg + pp][c:2 * c] for pp in range(ppg)], axis=1)
            o_all = qs * e_gc[i][:, g * hg * dk:(g + 1) * hg * dk] + pu[j]
            for hh in range(hg):
                h = g * hg + hh
                o = o_all[:, hh * dk:(hh + 1) * dk]
                ms = jnp.mean(o * o, axis=-1, keepdims=True)
                y_scr[h, sls[i], :] = o * lax.rsqrt(ms + RMS_EPS) * ng * z_ref[h, sls[i], :]
        return carry

    lax.fori_loop(0, NB // spi, iteration, 0)
    for h in range(n_heads):
        y_ref[:, h * dk:(h + 1) * dk] = y_scr[h].astype(bf16)

    @pl.when(c_idx == pl.num_programs(1) - 1)
    def _():
        for n in range(NB):
            for h in range(n_heads):
                o = (h % 2) * dk
                so_ref[n, h] = s_scr[n, h // 2, o:o + dk, o:o + dk]


def _mix_out_kernel(x_ref, xb_ref, ya_ref, yb_ref, yd_ref, wc_ref, stc_ref, ccw_ref, ccb_ref, cg_ref,
                    cbeta_ref, wg_ref, bg_ref, wbr_ref, wo_ref, g_ref, b_ref,
                    xo_ref, xbo_ref, stoc_ref, cbc, *, alpha):
    t_idx = pl.program_id(1)
    rows, d = x_ref.shape
    w = ya_ref.shape[1]
    _halo_init(cbc, stc_ref, t_idx)
    xb = xb_ref[...]

    pc = _dot(xb, wc_ref[...])
    glu = pc[:, 0:w] * jax.nn.sigmoid(pc[:, w:2 * w])
    conv = _causal_conv(cbc, glu, ccw_ref[...], stc_ref.shape[0]) + ccb_ref[...]
    _halo_roll(cbc, stoc_ref, rows)
    y_c = (_silu(_layer_norm(conv, cg_ref[...], cbeta_ref[...])) * _silu(pc[:, 2 * w:3 * w])).astype(bf16)

    gates = jax.nn.sigmoid(_dot(xb, wg_ref[...]) + bg_ref[...])
    mixed = None
    for n, y in enumerate((ya_ref[...], yb_ref[...], y_c, yd_ref[...])):
        t = gates[:, n * d:(n + 1) * d] * _dot(y, wbr_ref[n])
        mixed = t if mixed is None else mixed + t
    out = _dot(mixed.astype(bf16), wo_ref[...])
    xn = _layer_norm(alpha * x_ref[...] + out, g_ref[...], b_ref[...])
    xo_ref[...] = xn
    xbo_ref[...] = xn.astype(bf16)


def _tile_spec(rows, width, n_t):
    return pl.BlockSpec((rows, width), lambda g, t: (g * n_t + t, 0))


def _group_spec(shape):
    n = len(shape)
    return pl.BlockSpec((None,) + tuple(shape), lambda g, t: (g,) + (0,) * n)


def _const_spec(shape):
    n = len(shape)
    return pl.BlockSpec(tuple(shape), lambda g, t: (0,) * n, pipeline_mode=pl.Buffered(1))


def _layer(x, xb, st, p, *, n_groups, seq_len, tt, alpha):
    conv_a, conv_b, lru, conv_c, conv_d, delta = st
    t_rows, d = x.shape
    w = p["a_conv_w"].shape[1]
    n_t = seq_len // tt
    rows = tt * NB
    grid = (n_groups, n_t)
    tile_in = _tile_spec(rows, d, n_t)
    tile_w = _tile_spec(rows, w, n_t)
    tile_gb = _tile_spec(rows, LANES, n_t)
    act16 = jax.ShapeDtypeStruct((t_rows, w), bf16)

    def shape_of(a):
        return jax.ShapeDtypeStruct(a.shape, f32)

    def grp(a):
        return _group_spec(a.shape[1:])

    def const(name):
        return _const_spec(p[name].shape)

    def cbuf(a):
        return pltpu.VMEM((a.shape[1] + rows, a.shape[2]), f32)

    n_heads, dk, dv = delta.shape[1:]
    assert dk == LANES and dv == LANES
    head_rows = jax.ShapeDtypeStruct((n_heads, t_rows, dk), f32)
    head_tile = pl.BlockSpec((n_heads, rows, dk), lambda g, t: (0, g * n_t + t, 0))
    in_names = ("w_a", "w_b", "w_d", "w_dab")
    par_names = ("a_conv_w", "b_conv_w", "b_conv_b", "b_wxa", "b_bx", "b_ba", "b_lambda", "d_conv_w",
                 "d_par")
    y_a, y_b, q, k, v, z, gb, new_a, new_b, new_lru, new_d = pl.pallas_call(
        _mix_in_kernel,
        grid=grid,
        in_specs=[tile_in] + [const(n) for n in in_names] + [grp(conv_a), grp(conv_b), grp(lru), grp(conv_d)]
        + [const(n) for n in par_names],
        out_specs=[tile_w, tile_w] + [head_tile] * 4 + [tile_gb, grp(conv_a), grp(conv_b), grp(lru),
                                                        grp(conv_d)],
        out_shape=[act16, act16] + [head_rows] * 4 + [jax.ShapeDtypeStruct((t_rows, LANES), f32),
                                                      shape_of(conv_a), shape_of(conv_b), shape_of(lru),
                                                      shape_of(conv_d)],
        scratch_shapes=[cbuf(conv_a), cbuf(conv_b), cbuf(conv_d), pltpu.VMEM(lru.shape[1:], f32)],
        compiler_params=_params(2),
        name="mix_in",
    )(xb, *[p[n] for n in in_names], conv_a, conv_b, lru, conv_d, *[p[n] for n in par_names])

    chunk = min(GDN_CHUNK, seq_len)
    n_c = seq_len // chunk
    c_rows = chunk * NB
    head_spec = pl.BlockSpec((n_heads, c_rows, dk), lambda g, c: (0, g * n_c + c, 0))
    state_spec = pl.BlockSpec((NB, n_heads, dk, dv), lambda g, c: (g, 0, 0, 0))
    y_d, new_delta = pl.pallas_call(
        functools.partial(_delta_kernel, n_heads=n_heads, seqs_per_iter=DELTA_SEQS),
        grid=(n_groups, n_c),
        in_specs=[head_spec, head_spec, head_spec, head_spec, _tile_spec(c_rows, LANES, n_c),
                  state_spec, const("d_norm_g")],
        out_specs=[_tile_spec(c_rows, w, n_c), state_spec],
        out_shape=[act16, shape_of(delta)],
        scratch_shapes=[pltpu.VMEM((NB, n_heads // 2, PACK, PACK), f32),
                        pltpu.VMEM((n_heads, c_rows, dk), f32)],
        compiler_params=_params(2),
        name="delta",
    )(q, k, v, z, gb, delta, p["d_norm_g"])

    out_names = ("c_conv_w", "c_conv_b", "c_ln_g", "c_ln_b", "w_g", "b_gate", "w_branch", "w_out",
                 "ln_g", "ln_b")
    x_new, xb_new, new_c = pl.pallas_call(
        functools.partial(_mix_out_kernel, alpha=alpha),
        grid=grid,
        in_specs=[tile_in, tile_in, tile_w, tile_w, tile_w, const("w_c"), grp(conv_c)]
        + [const(n) for n in out_names],
        out_specs=[tile_in, tile_in, grp(conv_c)],
        out_shape=[jax.ShapeDtypeStruct((t_rows, d), f32), jax.ShapeDtypeStruct((t_rows, d), bf16),
                   shape_of(conv_c)],
        scratch_shapes=[cbuf(conv_c)],
        compiler_params=_params(2),
        name="mix_out",
    )(x, xb, y_a, y_b, y_d, p["w_c"], conv_c, *[p[n] for n in out_names])

    return x_new, xb_new, (new_a, new_b, new_lru, new_c, new_d, new_delta)


def _rows_from_seq(x):
    b, l, d = x.shape
    return x.reshape(b // NB, NB, l, d).transpose(0, 2, 1, 3).reshape(b * l, d)


def _seq_from_rows(x, b, l):
    d = x.shape[-1]
    return x.reshape(b // NB, l, NB, d).transpose(0, 2, 1, 3).reshape(b, l, d)


def _conv_state_to_rows(s):
    b, k1, c = s.shape
    return s.reshape(b // NB, NB, k1, c).transpose(0, 2, 1, 3).reshape(b // NB, k1 * NB, c)


def _conv_state_from_rows(s, b):
    g, r, c = s.shape
    k1 = r // NB
    return s.reshape(g, k1, NB, c).transpose(0, 2, 1, 3).reshape(b, k1, c)


def _run_trunk(x, states, ln_in_g, ln_in_b, params, depth, alpha):
    b, l, d = x.shape
    assert b % NB == 0
    n_groups = b // NB
    tt = min(l, TILE_FRAMES)
    assert l % tt == 0 and l % min(GDN_CHUNK, l) == 0
    conv_a, conv_b, lru, conv_c, conv_d, delta = states
    rows = _rows_from_seq(x)
    xc, xbc = _ln_in(rows, ln_in_g, ln_in_b, tt * NB)
    new = []
    for i in range(depth):
        st = (_conv_state_to_rows(conv_a[i]), _conv_state_to_rows(conv_b[i]),
              lru[i].reshape(n_groups, NB, -1), _conv_state_to_rows(conv_c[i]),
              _conv_state_to_rows(conv_d[i]), delta[i])
        p = {name: arr[i] for name, arr in params.items()}
        xc, xbc, (na, nb_, nl, nc, nd, ndelta) = _layer(xc, xbc, st, p, n_groups=n_groups, seq_len=l,
                                                        tt=tt, alpha=alpha)
        new.append((_conv_state_from_rows(na, b), _conv_state_from_rows(nb_, b), nl.reshape(b, -1),
                    _conv_state_from_rows(nc, b), _conv_state_from_rows(nd, b), ndelta))
    return _seq_from_rows(xc, b, l), tuple(jnp.stack(s) for s in zip(*new))


def kernel(x_prompt, x_sample, state_conv_a, state_conv_b, state_lru, state_conv_c, state_conv_d, state_delta, ln_in_g, ln_in_b, w_in, b_gate, a_conv_w, b_conv_w, b_conv_b, b_wx, b_bx, b_wa, b_ba, b_lambda, c_conv_w, c_conv_b, c_ln_g, c_ln_b, d_conv_w, d_a_log, d_dt_bias, d_norm_g, w_branch, w_out, ln_g, ln_b):
    depth, d, _ = w_in.shape
    w_a, w_b, w_c = a_conv_w.shape[2], b_conv_w.shape[2], c_conv_w.shape[2]
    w_qkv = d_conv_w.shape[2]
    n_heads, dv = d_a_log.shape[1], d_norm_g.shape[1]
    w_d = n_heads * dv
    assert n_heads <= LANES // 2
    off_b = 4 * w_a
    off_c = off_b + 2 * w_b
    off_d = off_c + 3 * w_c
    off_da = off_d + w_qkv + w_d
    off_g = off_da + 2 * n_heads
    alpha = (2 * depth) ** 0.25

    def row(a):
        return a.reshape(depth, 1, -1)

    def cols(lo, hi):
        return w_in[:, :, lo:hi].astype(bf16)

    d_par = jnp.zeros((depth, 8, LANES), f32)
    d_par = d_par.at[:, 0, :n_heads].set(d_a_log).at[:, 1, :n_heads].set(d_dt_bias)
    params = dict(
        w_a=cols(0, off_b), w_b=cols(off_b, off_c), w_c=cols(off_c, off_d),
        w_d=cols(off_d, off_da),
        w_dab=jnp.pad(w_in[:, :, off_da:off_g], ((0, 0), (0, 0), (0, LANES - 2 * n_heads))).astype(bf16),
        w_g=cols(off_g, off_g + b_gate.shape[1] * d),
        b_gate=row(b_gate), a_conv_w=a_conv_w, b_conv_w=b_conv_w, b_conv_b=row(b_conv_b),
        b_wxa=jnp.concatenate([b_wx, b_wa], axis=-1).astype(bf16),
        b_bx=row(b_bx), b_ba=row(b_ba), b_lambda=row(b_lambda),
        c_conv_w=c_conv_w, c_conv_b=row(c_conv_b), c_ln_g=row(c_ln_g), c_ln_b=row(c_ln_b),
        d_conv_w=d_conv_w, d_par=d_par, d_norm_g=row(d_norm_g),
        w_branch=w_branch.astype(bf16), w_out=w_out.astype(bf16), ln_g=row(ln_g), ln_b=row(ln_b),
    )
    g_in, b_in = ln_in_g.reshape(1, d), ln_in_b.reshape(1, d)

    bp = x_prompt.shape[0]

    def zeros(s):
        return jnp.zeros((depth, bp) + s.shape[2:], s.dtype)

    prompt_states = tuple(zeros(s) for s in (state_conv_a, state_conv_b, state_lru, state_conv_c,
                                              state_conv_d, state_delta))
    y_prompt, p_st = _run_trunk(x_prompt, prompt_states, g_in, b_in, params, depth, alpha)
    sample_states = (state_conv_a, state_conv_b, state_lru, state_conv_c, state_conv_d, state_delta)
    y_sample, s_st = _run_trunk(x_sample, sample_states, g_in, b_in, params, depth, alpha)
    return (y_prompt, y_sample) + tuple(p_st) + tuple(s_st)
```

```python
import functools

import jax
import jax.numpy as jnp
from jax import lax
from jax.experimental import pallas as pl
from jax.experimental.pallas import tpu as pltpu

f32 = jnp.float32
bf16 = jnp.bfloat16

LN_EPS = 1e-5
RMS_EPS = 1e-6
L2_EPS = 1e-6
LRU_C = 8.0
GDN_CHUNK = 64
NB = 8
LANES = 128
PACK = 2 * LANES
DELTA_SEQS = 4
TILE_FRAMES = 32
VMEM_LIMIT = 56 * 1024 * 1024


def _sigmoid(x):
    return 0.5 * jnp.tanh(0.5 * x) + 0.5


def _silu(x):
    h = 0.5 * x
    return h * jnp.tanh(h) + h


def _softplus(x):
    return jnp.maximum(x, 0.0) + jnp.log1p(jnp.exp(-jnp.abs(x)))


def _layer_norm(x, g, b):
    mu = jnp.mean(x, axis=-1, keepdims=True)
    xc = x - mu
    var = jnp.mean(xc * xc, axis=-1, keepdims=True)
    return xc * lax.rsqrt(var + LN_EPS) * g + b


def _dot(a, b):
    return jnp.dot(a, b, preferred_element_type=f32)


def _pack_rows(w):
    *lead, k, n = w.shape
    pairs = w.astype(bf16).reshape(*lead, k // 2, 2, n)
    return lax.bitcast_convert_type(jnp.swapaxes(pairs, -1, -2), jnp.uint32)


def _wdot(a, w_packed):
    return _dot(a, pltpu.bitcast(w_packed, bf16))


def _params(n_axes):
    return pltpu.CompilerParams(dimension_semantics=("arbitrary",) * n_axes,
                                vmem_limit_bytes=VMEM_LIMIT)


def _halo_init(cbuf, st_ref, t_idx):
    @pl.when(t_idx == 0)
    def _():
        cbuf[0:st_ref.shape[0], :] = st_ref[...]


def _causal_conv(cbuf, u, cw, halo):
    rows = u.shape[0]
    cbuf[halo:halo + rows, :] = u
    acc = None
    for k in range(cw.shape[0]):
        t = cbuf[k * NB:k * NB + rows, :] * cw[k:k + 1, :]
        acc = t if acc is None else acc + t
    return acc


def _halo_roll(cbuf, sto_ref, rows):
    halo = sto_ref.shape[0]
    new = cbuf[rows:rows + halo, :]
    cbuf[0:halo, :] = new
    sto_ref[...] = new


def _ln_in_kernel(x_ref, g_ref, b_ref, xo_ref, xb_ref):
    y = _layer_norm(x_ref[...], g_ref[...], b_ref[...])
    xo_ref[...] = y
    xb_ref[...] = y.astype(bf16)


def _ln_in(x, g, b, rows):
    t, d = x.shape
    tile = pl.BlockSpec((rows, d), lambda i: (i, 0))
    vec = pl.BlockSpec((1, d), lambda i: (0, 0))
    return pl.pallas_call(
        _ln_in_kernel,
        grid=(t // rows,),
        in_specs=[tile, vec, vec],
        out_specs=[tile, tile],
        out_shape=[jax.ShapeDtypeStruct((t, d), f32), jax.ShapeDtypeStruct((t, d), bf16)],
        compiler_params=_params(1),
        name="ln_in",
    )(x, g, b)


def _mix_in_kernel(xb_ref, wa_ref, wb_ref, wd_ref, wdab_ref, sta_ref, stb_ref, h0_ref, std_ref,
                   acw_ref, bcw_ref, bcb_ref, wxa_ref, bx_ref, ba_ref, lam_ref, dcw_ref, dpar_ref,
                   ya_ref, yb_ref, q_ref, k_ref, v_ref, z_ref, gb_ref,
                   stoa_ref, stob_ref, ho_ref, stod_ref,
                   cba, cbb, cbd, h_scr):
    t_idx = pl.program_id(1)
    rows, w = ya_ref.shape
    n_heads, _, dk = q_ref.shape
    n_blk, bw, _ = wxa_ref.shape
    _halo_init(cba, sta_ref, t_idx)
    _halo_init(cbb, stb_ref, t_idx)
    _halo_init(cbd, std_ref, t_idx)

    @pl.when(t_idx == 0)
    def _():
        h_scr[...] = h0_ref[...]

    xb = xb_ref[...]

    pa = _wdot(xb, wa_ref[...])
    a_b, a_c, a_x, a_z = (pa[:, n * w:(n + 1) * w] for n in range(4))
    conv = _causal_conv(cba, a_c * a_x, acw_ref[...], sta_ref.shape[0])
    ya_ref[...] = (a_b * conv * _silu(a_z)).astype(bf16)
    _halo_roll(cba, stoa_ref, rows)

    pb = _wdot(xb, wb_ref[...])
    xc = _causal_conv(cbb, pb[:, 0:w], bcw_ref[...], stb_ref.shape[0]) + bcb_ref[...]
    _halo_roll(cbb, stob_ref, rows)
    xc16 = xc.astype(bf16)
    pre = [_dot(xc16[:, h * bw:(h + 1) * bw], wxa_ref[h]) for h in range(n_blk)]
    gate_x = _sigmoid(jnp.concatenate([g[:, 0:bw] for g in pre], axis=1) + bx_ref[...])
    gate_a = _sigmoid(jnp.concatenate([g[:, bw:2 * bw] for g in pre], axis=1) + ba_ref[...])
    a = jnp.exp(gate_a * (-LRU_C * _softplus(-lam_ref[...])))
    bterm = jnp.sqrt(1.0 - a * a) * gate_x * xc
    h = h_scr[...]
    hs = []
    for s in range(rows // NB):
        h = a[s * NB:(s + 1) * NB, :] * h + bterm[s * NB:(s + 1) * NB, :]
        hs.append(h)
    h_scr[...] = h
    ho_ref[...] = h
    yb_ref[...] = (jnp.concatenate(hs, axis=0) * _silu(pb[:, w:2 * w])).astype(bf16)

    pd = _wdot(xb, wd_ref[...])
    pab = _dot(xb, wdab_ref[...])
    qkv = _silu(_causal_conv(cbd, pd[:, 0:3 * w], dcw_ref[...], std_ref.shape[0]))
    _halo_roll(cbd, stod_ref, rows)

    def l2n(xh, scale):
        ss = jnp.sum(xh * xh, axis=-1, keepdims=True)
        return xh * (lax.rsqrt(ss + L2_EPS) * scale)

    for hd in range(n_heads):
        q_ref[hd] = l2n(qkv[:, hd * dk:(hd + 1) * dk], dk ** -0.5)
        k_ref[hd] = l2n(qkv[:, w + hd * dk:w + (hd + 1) * dk], 1.0)
        v_ref[hd] = qkv[:, 2 * w + hd * dk:2 * w + (hd + 1) * dk]
        z_ref[hd] = _silu(pd[:, 3 * w + hd * dk:3 * w + (hd + 1) * dk])
    g = -jnp.exp(dpar_ref[0:1, :]) * _softplus(pab + dpar_ref[1:2, :])
    lane = lax.broadcasted_iota(jnp.int32, pab.shape, 1)
    gb_ref[...] = jnp.where(lane < n_heads, g, _sigmoid(pab))


def _split_bf16(x, parts):
    out = []
    for _ in range(parts):
        t = x.astype(bf16)
        out.append(t)
        x = x - t.astype(f32)
    return out


def _delta_kernel(q_ref, k_ref, v_ref, z_ref, gb_ref, s0_ref, ng_ref, y_ref, so_ref, s_scr, y_scr,
                  *, n_heads, seqs_per_iter):
    c_idx = pl.program_id(1)
    _, rows, dk = q_ref.shape
    c = rows // NB
    hg = PACK // c
    n_grp = n_heads // hg
    n_pair = n_heads // 2
    ppg = hg // 2
    spi = seqs_per_iter
    assert c * hg == PACK and n_grp * hg == n_heads and 2 * ppg == hg and 2 * dk == PACK

    @pl.when(c_idx == 0)
    def _():
        s_scr[...] = jnp.zeros(s_scr.shape, f32)
        for n in range(NB):
            for h in range(n_heads):
                o = (h % 2) * dk
                s_scr[n, h // 2, o:o + dk, o:o + dk] = s0_ref[n, h]

    def iota(shape, axis):
        return lax.broadcasted_iota(jnp.int32, shape, axis)

    def blk(i, size):
        return lax.shift_right_logical(i, size.bit_length() - 1)

    row = iota((c, PACK), 0)
    lane = iota((c, PACK), 1)
    jcol = lane & (c - 1)
    incl = row >= jcol
    strict = row > jcol
    eye_f = (row == jcol).astype(f32)
    head_sel = [(blk(lane, c) == hh).astype(bf16) for hh in range(hg)]
    base = blk(row, 8) == blk(jcol, 8)
    joins = []
    s = 8
    while s < c:
        joins.append((blk(row, 2 * s) == blk(jcol, 2 * s)) & (blk(row, s) != blk(jcol, s)))
        s *= 2
    ltri = (iota((c, c), 0) >= iota((c, c), 1)).astype(bf16)
    s_diag = blk(iota((PACK, PACK), 0), dk) == blk(iota((PACK, PACK), 1), dk)
    lane128 = iota((c, LANES), 1)
    is_g = lane128 < n_heads
    heads_per_tile = LANES // c
    zero16 = jnp.zeros((c, dk), bf16)
    ng = ng_ref[...]
    nt = (((1,), (1,)), ((), ()))
    tn = (((0,), (0,)), ((), ()))
    chains = [(i, g) for i in range(spi) for g in range(n_grp)]
    pairs = [(i, p) for i in range(spi) for p in range(n_pair)]

    def bd_heads(blocks):
        return jnp.concatenate(
            [jnp.concatenate([blocks[i] if i == j else zero16 for j in range(hg)], axis=1)
             for i in range(hg)], axis=0)

    def mmp(xs, ys):
        out = []
        for x, y in zip(xs, ys):
            y16 = y.astype(bf16)
            out.append(_dot(x.astype(bf16), jnp.concatenate([y16 * m for m in head_sel], axis=0)))
        return out

    def iteration(it, carry):
        seqs = [it * spi + i for i in range(spi)]
        sls = [pl.ds(n, c, stride=NB) for n in seqs]
        sq = range(spi)

        gbs = [gb_ref[sl, :] for sl in sls]
        cs = [_dot(ltri, jnp.concatenate(_split_bf16(jnp.where(is_g, gb, 0.0), 3), axis=1)) for gb in gbs]
        gc = [t[:, 0:LANES] + t[:, LANES:2 * LANES] + t[:, 2 * LANES:3 * LANES] for t in cs]

        def per_head_lanes(a, first):
            return jnp.concatenate([jnp.broadcast_to(a[:, first + h:first + h + 1], (c, dk))
                                    for h in range(n_heads)], axis=1)

        def packed_cols(full, g):
            tiles = []
            for t in range(PACK // LANES):
                first = g * hg + t * heads_per_tile
                val = full[:, first * dk:(first + 1) * dk]
                for m in range(1, heads_per_tile):
                    val = jnp.where(lane128 >= m * c, full[:, (first + m) * dk:(first + m + 1) * dk], val)
                tiles.append(val)
            return jnp.concatenate(tiles, axis=1)

        gc_full = [per_head_lanes(t, 0) for t in gc]
        beta_full = [per_head_lanes(t, n_heads) for t in gbs]
        gcol = [packed_cols(gc_full[i], g) for i, g in chains]
        grow = [jnp.sum(t * eye_f, axis=0, keepdims=True) for t in gcol]
        decay = [jnp.exp(jnp.where(incl, gcol[j] - grow[j], -jnp.inf)) for j in range(len(chains))]
        g_last = [t[c - 1:c, :] for t in gc_full]
        e_gc = [jnp.exp(t) for t in gc_full]
        e_rest = [jnp.exp(g_last[i] - gc_full[i]) for i in sq]
        e_last = [jnp.exp(t) for t in g_last]

        def heads16(a):
            return [a[:, h * dk:(h + 1) * dk].astype(bf16) for h in range(n_heads)]

        def load(ref, sl):
            return jnp.concatenate([ref[h, sl, :] for h in range(n_heads)], axis=1)

        qh = [heads16(load(q_ref, sl)) for sl in sls]
        k32 = [load(k_ref, sl) for sl in sls]
        kh = [heads16(t) for t in k32]
        kb32 = [k32[i] * beta_full[i] for i in sq]
        kbh = [heads16(t) for t in kb32]
        kbeh = [heads16(kb32[i] * e_gc[i]) for i in sq]
        vbh = [heads16(load(v_ref, sls[i]) * beta_full[i]) for i in sq]
        kbd = [bd_heads(kh[i][g * hg:(g + 1) * hg]) for i, g in chains]
        kq = [jnp.concatenate([jnp.concatenate(kbh[i][g * hg:(g + 1) * hg], axis=1),
                               jnp.concatenate(qh[i][g * hg:(g + 1) * hg], axis=1)], axis=0)
              for i, g in chains]
        kkqk = [lax.dot_general(kq[j], kbd[j], nt, preferred_element_type=f32) for j in range(len(chains))]
        a_mat = [jnp.where(strict, kkqk[j][0:c] * decay[j], 0.0) for j in range(len(chains))]
        p_mat = [(kkqk[j][c:2 * c] * decay[j]).astype(bf16) for j in range(len(chains))]

        a0 = [jnp.where(base, m, 0.0) for m in a_mat]
        a2 = mmp(a0, a0)
        x = [eye_f - m for m in a0]
        both = mmp([jnp.concatenate([p, q], axis=0) for p, q in zip(a2, x)], a2)
        a4 = [t[0:c] for t in both]
        x = [p + t[c:2 * c] for p, t in zip(x, both)]
        x = [p + q for p, q in zip(x, mmp(x, a4))]
        for join in joins:
            low = [jnp.where(join, m, 0.0) for m in a_mat]
            x = [p - q for p, q in zip(x, mmp(mmp(x, low), x))]

        def tile_of(arrs, i, p):
            lo = (p % ppg) * 2 * c
            return arrs[i * n_grp + p // ppg][:, lo // LANES * LANES:(lo // LANES + 1) * LANES]

        def pair_rows(p, row_a, row_b):
            first = ((p % ppg) * 2 * c % LANES) // c
            zeros = [zero16] * len(row_a)
            blocks = [row_a if r == first else row_b if r == first + 1 else zeros
                      for r in range(LANES // c)]
            return jnp.concatenate([jnp.concatenate(b, axis=1) for b in blocks], axis=0)

        x16 = [t.astype(bf16) for t in x]
        uw = [_dot(tile_of(x16, i, p),
                   pair_rows(p, [vbh[i][2 * p], zero16, kbeh[i][2 * p], zero16],
                             [zero16, vbh[i][2 * p + 1], zero16, kbeh[i][2 * p + 1]]))
              for i, p in pairs]
        s_old = [s_scr[seqs[i], p] for i, p in pairs]
        s16 = [t.astype(bf16) for t in s_old]
        k_p = [jnp.concatenate(kh[i][2 * p:2 * p + 2], axis=1) for i, p in pairs]
        wq = [_dot(jnp.concatenate([uw[m][:, PACK:2 * PACK].astype(bf16),
                                    jnp.concatenate(qh[i][2 * p:2 * p + 2], axis=1)], axis=0), s16[m])
              for m, (i, p) in enumerate(pairs)]
        u = [uw[m][:, 0:PACK] - wq[m][0:c] for m in range(len(pairs))]
        u16 = [t.astype(bf16) for t in u]
        pu = [_dot(tile_of(p_mat, i, p), pair_rows(p, [u16[m][:, 0:dk], zero16], [zero16, u16[m][:, dk:PACK]]))
              for m, (i, p) in enumerate(pairs)]
        ktu = [lax.dot_general(k_p[m], (u[m] * e_rest[i][:, p * PACK:(p + 1) * PACK]).astype(bf16), tn,
                               preferred_element_type=f32) for m, (i, p) in enumerate(pairs)]
        for m, (i, p) in enumerate(pairs):
            s_scr[seqs[i], p] = (s_old[m] * e_last[i][:, p * PACK:(p + 1) * PACK]
                                 + jnp.where(s_diag, ktu[m], 0.0))
            o_pair = wq[m][c:2 * c] * e_gc[i][:, p * PACK:(p + 1) * PACK] + pu[m]
            for hh in range(2):
                h = 2 * p + hh
                o = o_pair[:, hh * dk:(hh + 1) * dk]
                ms = jnp.mean(o * o, axis=-1, keepdims=True)
                y_scr[h, sls[i], :] = o * lax.rsqrt(ms + RMS_EPS) * ng * z_ref[h, sls[i], :]
        return carry

    lax.fori_loop(0, NB // spi, iteration, 0)
    for h in range(n_heads):
        y_ref[:, h * dk:(h + 1) * dk] = y_scr[h].astype(bf16)

    @pl.when(c_idx == pl.num_programs(1) - 1)
    def _():
        for n in range(NB):
            for h in range(n_heads):
                o = (h % 2) * dk
                so_ref[n, h] = s_scr[n, h // 2, o:o + dk, o:o + dk]


def _mix_out_kernel(x_ref, xb_ref, ya_ref, yb_ref, yd_ref, wc_ref, stc_ref, ccw_ref, ccb_ref, cg_ref,
                    cbeta_ref, wg_ref, bg_ref, wbr_ref, wo_ref, g_ref, b_ref,
                    xo_ref, xbo_ref, stoc_ref, cbc, *, alpha):
    t_idx = pl.program_id(1)
    rows, d = x_ref.shape
    w = ya_ref.shape[1]
    _halo_init(cbc, stc_ref, t_idx)
    xb = xb_ref[...]

    pc = _wdot(xb, wc_ref[...])
    glu = pc[:, 0:w] * _sigmoid(pc[:, w:2 * w])
    conv = _causal_conv(cbc, glu, ccw_ref[...], stc_ref.shape[0]) + ccb_ref[...]
    _halo_roll(cbc, stoc_ref, rows)
    y_c = (_silu(_layer_norm(conv, cg_ref[...], cbeta_ref[...])) * _silu(pc[:, 2 * w:3 * w])).astype(bf16)

    gates = _sigmoid(_wdot(xb, wg_ref[...]) + bg_ref[...])
    mixed = None
    for n, y in enumerate((ya_ref[...], yb_ref[...], y_c, yd_ref[...])):
        t = gates[:, n * d:(n + 1) * d] * _wdot(y, wbr_ref[n])
        mixed = t if mixed is None else mixed + t
    out = _wdot(mixed.astype(bf16), wo_ref[...])
    xn = _layer_norm(alpha * x_ref[...] + out, g_ref[...], b_ref[...])
    xo_ref[...] = xn
    xbo_ref[...] = xn.astype(bf16)


def _tile_spec(rows, width, n_t):
    return pl.BlockSpec((rows, width), lambda g, t: (g * n_t + t, 0))


def _group_spec(shape):
    n = len(shape)
    return pl.BlockSpec((None,) + tuple(shape), lambda g, t: (g,) + (0,) * n)


def _const_spec(shape):
    n = len(shape)
    return pl.BlockSpec(tuple(shape), lambda g, t: (0,) * n, pipeline_mode=pl.Buffered(1))


def _layer(x, xb, st, p, *, n_groups, seq_len, tt, alpha):
    conv_a, conv_b, lru, conv_c, conv_d, delta = st
    t_rows, d = x.shape
    w = p["a_conv_w"].shape[1]
    n_t = seq_len // tt
    rows = tt * NB
    grid = (n_groups, n_t)
    tile_in = _tile_spec(rows, d, n_t)
    tile_w = _tile_spec(rows, w, n_t)
    tile_gb = _tile_spec(rows, LANES, n_t)
    act16 = jax.ShapeDtypeStruct((t_rows, w), bf16)

    def shape_of(a):
        return jax.ShapeDtypeStruct(a.shape, f32)

    def grp(a):
        return _group_spec(a.shape[1:])

    def const(name):
        return _const_spec(p[name].shape)

    def cbuf(a):
        return pltpu.VMEM((a.shape[1] + rows, a.shape[2]), f32)

    n_heads, dk, dv = delta.shape[1:]
    assert dk == LANES and dv == LANES
    head_rows = jax.ShapeDtypeStruct((n_heads, t_rows, dk), f32)
    head_tile = pl.BlockSpec((n_heads, rows, dk), lambda g, t: (0, g * n_t + t, 0))
    in_names = ("w_a", "w_b", "w_d", "w_dab")
    par_names = ("a_conv_w", "b_conv_w", "b_conv_b", "b_wxa", "b_bx", "b_ba", "b_lambda", "d_conv_w",
                 "d_par")
    y_a, y_b, q, k, v, z, gb, new_a, new_b, new_lru, new_d = pl.pallas_call(
        _mix_in_kernel,
        grid=grid,
        in_specs=[tile_in] + [const(n) for n in in_names] + [grp(conv_a), grp(conv_b), grp(lru), grp(conv_d)]
        + [const(n) for n in par_names],
        out_specs=[tile_w, tile_w] + [head_tile] * 4 + [tile_gb, grp(conv_a), grp(conv_b), grp(lru),
                                                        grp(conv_d)],
        out_shape=[act16, act16] + [head_rows] * 4 + [jax.ShapeDtypeStruct((t_rows, LANES), f32),
                                                      shape_of(conv_a), shape_of(conv_b), shape_of(lru),
                                                      shape_of(conv_d)],
        scratch_shapes=[cbuf(conv_a), cbuf(conv_b), cbuf(conv_d), pltpu.VMEM(lru.shape[1:], f32)],
        compiler_params=_params(2),
        name="mix_in",
    )(xb, *[p[n] for n in in_names], conv_a, conv_b, lru, conv_d, *[p[n] for n in par_names])

    chunk = min(GDN_CHUNK, seq_len)
    n_c = seq_len // chunk
    c_rows = chunk * NB
    head_spec = pl.BlockSpec((n_heads, c_rows, dk), lambda g, c: (0, g * n_c + c, 0))
    state_spec = pl.BlockSpec((NB, n_heads, dk, dv), lambda g, c: (g, 0, 0, 0))
    y_d, new_delta = pl.pallas_call(
        functools.partial(_delta_kernel, n_heads=n_heads, seqs_per_iter=DELTA_SEQS),
        grid=(n_groups, n_c),
        in_specs=[head_spec, head_spec, head_spec, head_spec, _tile_spec(c_rows, LANES, n_c),
                  state_spec, const("d_norm_g")],
        out_specs=[_tile_spec(c_rows, w, n_c), state_spec],
        out_shape=[act16, shape_of(delta)],
        scratch_shapes=[pltpu.VMEM((NB, n_heads // 2, PACK, PACK), f32),
                        pltpu.VMEM((n_heads, c_rows, dk), f32)],
        compiler_params=_params(2),
        name="delta",
    )(q, k, v, z, gb, delta, p["d_norm_g"])

    out_names = ("c_conv_w", "c_conv_b", "c_ln_g", "c_ln_b", "w_g", "b_gate", "w_branch", "w_out",
                 "ln_g", "ln_b")
    x_new, xb_new, new_c = pl.pallas_call(
        functools.partial(_mix_out_kernel, alpha=alpha),
        grid=grid,
        in_specs=[tile_in, tile_in, tile_w, tile_w, tile_w, const("w_c"), grp(conv_c)]
        + [const(n) for n in out_names],
        out_specs=[tile_in, tile_in, grp(conv_c)],
        out_shape=[jax.ShapeDtypeStruct((t_rows, d), f32), jax.ShapeDtypeStruct((t_rows, d), bf16),
                   shape_of(conv_c)],
        scratch_shapes=[cbuf(conv_c)],
        compiler_params=_params(2),
        name="mix_out",
    )(x, xb, y_a, y_b, y_d, p["w_c"], conv_c, *[p[n] for n in out_names])

    return x_new, xb_new, (new_a, new_b, new_lru, new_c, new_d, new_delta)


def _rows_from_seq(x):
    b, l, d = x.shape
    return x.reshape(b // NB, NB, l, d).transpose(0, 2, 1, 3).reshape(b * l, d)


def _seq_from_rows(x, b, l):
    d = x.shape[-1]
    return x.reshape(b // NB, l, NB, d).transpose(0, 2, 1, 3).reshape(b, l, d)


def _conv_state_to_rows(s):
    b, k1, c = s.shape
    return s.reshape(b // NB, NB, k1, c).transpose(0, 2, 1, 3).reshape(b // NB, k1 * NB, c)


def _conv_state_from_rows(s, b):
    g, r, c = s.shape
    k1 = r // NB
    return s.reshape(g, k1, NB, c).transpose(0, 2, 1, 3).reshape(b, k1, c)


def _run_trunk(x, states, ln_in_g, ln_in_b, params, depth, alpha):
    b, l, d = x.shape
    assert b % NB == 0
    n_groups = b // NB
    tt = min(l, TILE_FRAMES)
    assert l % tt == 0 and l % min(GDN_CHUNK, l) == 0
    conv_a, conv_b, lru, conv_c, conv_d, delta = states
    rows = _rows_from_seq(x)
    xc, xbc = _ln_in(rows, ln_in_g, ln_in_b, tt * NB)
    new = []
    for i in range(depth):
        st = (_conv_state_to_rows(conv_a[i]), _conv_state_to_rows(conv_b[i]),
              lru[i].reshape(n_groups, NB, -1), _conv_state_to_rows(conv_c[i]),
              _conv_state_to_rows(conv_d[i]), delta[i])
        p = {name: arr[i] for name, arr in params.items()}
        xc, xbc, (na, nb_, nl, nc, nd, ndelta) = _layer(xc, xbc, st, p, n_groups=n_groups, seq_len=l,
                                                        tt=tt, alpha=alpha)
        new.append((_conv_state_from_rows(na, b), _conv_state_from_rows(nb_, b), nl.reshape(b, -1),
                    _conv_state_from_rows(nc, b), _conv_state_from_rows(nd, b), ndelta))
    return _seq_from_rows(xc, b, l), tuple(jnp.stack(s) for s in zip(*new))


def kernel(x_prompt, x_sample, state_conv_a, state_conv_b, state_lru, state_conv_c, state_conv_d, state_delta, ln_in_g, ln_in_b, w_in, b_gate, a_conv_w, b_conv_w, b_conv_b, b_wx, b_bx, b_wa, b_ba, b_lambda, c_conv_w, c_conv_b, c_ln_g, c_ln_b, d_conv_w, d_a_log, d_dt_bias, d_norm_g, w_branch, w_out, ln_g, ln_b):
    depth, d, _ = w_in.shape
    w_a, w_b, w_c = a_conv_w.shape[2], b_conv_w.shape[2], c_conv_w.shape[2]
    w_qkv = d_conv_w.shape[2]
    n_heads, dv = d_a_log.shape[1], d_norm_g.shape[1]
    w_d = n_heads * dv
    assert n_heads <= LANES // 2
    off_b = 4 * w_a
    off_c = off_b + 2 * w_b
    off_d = off_c + 3 * w_c
    off_da = off_d + w_qkv + w_d
    off_g = off_da + 2 * n_heads
    alpha = (2 * depth) ** 0.25

    def row(a):
        return a.reshape(depth, 1, -1)

    def cols(lo, hi):
        return _pack_rows(w_in[:, :, lo:hi])

    d_par = jnp.zeros((depth, 8, LANES), f32)
    d_par = d_par.at[:, 0, :n_heads].set(d_a_log).at[:, 1, :n_heads].set(d_dt_bias)
    params = dict(
        w_a=cols(0, off_b), w_b=cols(off_b, off_c), w_c=cols(off_c, off_d),
        w_d=cols(off_d, off_da),
        w_dab=jnp.pad(w_in[:, :, off_da:off_g], ((0, 0), (0, 0), (0, LANES - 2 * n_heads))).astype(bf16),
        w_g=cols(off_g, off_g + b_gate.shape[1] * d),
        b_gate=row(b_gate), a_conv_w=a_conv_w, b_conv_w=b_conv_w, b_conv_b=row(b_conv_b),
        b_wxa=jnp.concatenate([b_wx, b_wa], axis=-1).astype(bf16),
        b_bx=row(b_bx), b_ba=row(b_ba), b_lambda=row(b_lambda),
        c_conv_w=c_conv_w, c_conv_b=row(c_conv_b), c_ln_g=row(c_ln_g), c_ln_b=row(c_ln_b),
        d_conv_w=d_conv_w, d_par=d_par, d_norm_g=row(d_norm_g),
        w_branch=_pack_rows(w_branch), w_out=_pack_rows(w_out), ln_g=row(ln_g), ln_b=row(ln_b),
    )
    g_in, b_in = ln_in_g.reshape(1, d), ln_in_b.reshape(1, d)

    bp = x_prompt.shape[0]

    def zeros(s):
        return jnp.zeros((depth, bp) + s.shape[2:], s.dtype)

    prompt_states = tuple(zeros(s) for s in (state_conv_a, state_conv_b, state_lru, state_conv_c,
                                              state_conv_d, state_delta))
    y_prompt, p_st = _run_trunk(x_prompt, prompt_states, g_in, b_in, params, depth, alpha)
    sample_states = (state_conv_a, state_conv_b, state_lru, state_conv_c, state_conv_d, state_delta)
    y_sample, s_st = _run_trunk(x_sample, sample_states, g_in, b_in, params, depth, alpha)
    return (y_prompt, y_sample) + tuple(p_st) + tuple(s_st)
```

```python
import functools

import jax
import jax.numpy as jnp
from jax import lax
from jax.experimental import pallas as pl
from jax.experimental.pallas import tpu as pltpu

f32 = jnp.float32
bf16 = jnp.bfloat16

LN_EPS = 1e-5
RMS_EPS = 1e-6
L2_EPS = 1e-6
LRU_C = 8.0
GDN_CHUNK = 64
NB = 8
LANES = 128
PACK = 2 * LANES
DELTA_SEQS = 4
TILE_FRAMES = 32
VMEM_LIMIT = 56 * 1024 * 1024


def _sigmoid(x):
    return 0.5 * jnp.tanh(0.5 * x) + 0.5


def _silu(x):
    h = 0.5 * x
    return h * jnp.tanh(h) + h


def _softplus(x):
    return jnp.maximum(x, 0.0) + jnp.log1p(jnp.exp(-jnp.abs(x)))


def _layer_norm(x, g, b):
    mu = jnp.mean(x, axis=-1, keepdims=True)
    xc = x - mu
    var = jnp.mean(xc * xc, axis=-1, keepdims=True)
    return xc * lax.rsqrt(var + LN_EPS) * g + b


def _dot(a, b):
    return jnp.dot(a, b, preferred_element_type=f32)


def _params(n_axes):
    return pltpu.CompilerParams(dimension_semantics=("arbitrary",) * n_axes,
                                vmem_limit_bytes=VMEM_LIMIT)


PACK_ROWS = 256


def _pack_rows_kernel(w_ref, o_ref):
    o_ref[...] = pltpu.bitcast(w_ref[...].astype(bf16), jnp.uint32)


def _pack_rows(w):
    lead, k, n = w.shape
    return pl.pallas_call(
        _pack_rows_kernel,
        grid=(lead, k // PACK_ROWS),
        in_specs=[pl.BlockSpec((None, PACK_ROWS, n), lambda l, i: (l, i, 0))],
        out_specs=pl.BlockSpec((None, PACK_ROWS // 2, n), lambda l, i: (l, i, 0)),
        out_shape=jax.ShapeDtypeStruct((lead, k // 2, n), jnp.uint32),
        compiler_params=_params(2),
        name="pack_rows",
    )(w)


def _wdot(a, w_packed):
    return _dot(a, pltpu.bitcast(w_packed, bf16))


def _halo_init(cbuf, st_ref, t_idx):
    @pl.when(t_idx == 0)
    def _():
        cbuf[0:st_ref.shape[0], :] = st_ref[...]


def _causal_conv(cbuf, u, cw, halo):
    rows = u.shape[0]
    cbuf[halo:halo + rows, :] = u
    acc = None
    for k in range(cw.shape[0]):
        t = cbuf[k * NB:k * NB + rows, :] * cw[k:k + 1, :]
        acc = t if acc is None else acc + t
    return acc


def _halo_roll(cbuf, sto_ref, rows):
    halo = sto_ref.shape[0]
    new = cbuf[rows:rows + halo, :]
    cbuf[0:halo, :] = new
    sto_ref[...] = new


def _ln_in_kernel(x_ref, g_ref, b_ref, xo_ref, xb_ref):
    y = _layer_norm(x_ref[...], g_ref[...], b_ref[...])
    xo_ref[...] = y
    xb_ref[...] = y.astype(bf16)


def _ln_in(x, g, b, rows):
    t, d = x.shape
    tile = pl.BlockSpec((rows, d), lambda i: (i, 0))
    vec = pl.BlockSpec((1, d), lambda i: (0, 0))
    return pl.pallas_call(
        _ln_in_kernel,
        grid=(t // rows,),
        in_specs=[tile, vec, vec],
        out_specs=[tile, tile],
        out_shape=[jax.ShapeDtypeStruct((t, d), f32), jax.ShapeDtypeStruct((t, d), bf16)],
        compiler_params=_params(1),
        name="ln_in",
    )(x, g, b)


def _mix_in_kernel(xb_ref, wa_ref, wb_ref, wd_ref, wdab_ref, sta_ref, stb_ref, h0_ref, std_ref,
                   acw_ref, bcw_ref, bcb_ref, wxa_ref, bx_ref, ba_ref, lam_ref, dcw_ref, dpar_ref,
                   ya_ref, yb_ref, q_ref, k_ref, v_ref, z_ref, gb_ref,
                   stoa_ref, stob_ref, ho_ref, stod_ref,
                   cba, cbb, cbd, h_scr):
    t_idx = pl.program_id(1)
    rows, w = ya_ref.shape
    n_heads, _, dk = q_ref.shape
    n_blk, bw, _ = wxa_ref.shape
    _halo_init(cba, sta_ref, t_idx)
    _halo_init(cbb, stb_ref, t_idx)
    _halo_init(cbd, std_ref, t_idx)

    @pl.when(t_idx == 0)
    def _():
        h_scr[...] = h0_ref[...]

    xb = xb_ref[...]

    pa = _wdot(xb, wa_ref[...])
    a_b, a_c, a_x, a_z = (pa[:, n * w:(n + 1) * w] for n in range(4))
    conv = _causal_conv(cba, a_c * a_x, acw_ref[...], sta_ref.shape[0])
    ya_ref[...] = (a_b * conv * _silu(a_z)).astype(bf16)
    _halo_roll(cba, stoa_ref, rows)

    pb = _wdot(xb, wb_ref[...])
    xc = _causal_conv(cbb, pb[:, 0:w], bcw_ref[...], stb_ref.shape[0]) + bcb_ref[...]
    _halo_roll(cbb, stob_ref, rows)
    xc16 = xc.astype(bf16)
    pre = [_dot(xc16[:, h * bw:(h + 1) * bw], wxa_ref[h]) for h in range(n_blk)]
    gate_x = _sigmoid(jnp.concatenate([g[:, 0:bw] for g in pre], axis=1) + bx_ref[...])
    gate_a = _sigmoid(jnp.concatenate([g[:, bw:2 * bw] for g in pre], axis=1) + ba_ref[...])
    a = jnp.exp(gate_a * (-LRU_C * _softplus(-lam_ref[...])))
    bterm = jnp.sqrt(1.0 - a * a) * gate_x * xc
    h = h_scr[...]
    hs = []
    for s in range(rows // NB):
        h = a[s * NB:(s + 1) * NB, :] * h + bterm[s * NB:(s + 1) * NB, :]
        hs.append(h)
    h_scr[...] = h
    ho_ref[...] = h
    yb_ref[...] = (jnp.concatenate(hs, axis=0) * _silu(pb[:, w:2 * w])).astype(bf16)

    pd = _wdot(xb, wd_ref[...])
    pab = _dot(xb, wdab_ref[...])
    qkv = _silu(_causal_conv(cbd, pd[:, 0:3 * w], dcw_ref[...], std_ref.shape[0]))
    _halo_roll(cbd, stod_ref, rows)

    def l2n(xh, scale):
        ss = jnp.sum(xh * xh, axis=-1, keepdims=True)
        return xh * (lax.rsqrt(ss + L2_EPS) * scale)

    for hd in range(n_heads):
        q_ref[hd] = l2n(qkv[:, hd * dk:(hd + 1) * dk], dk ** -0.5)
        k_ref[hd] = l2n(qkv[:, w + hd * dk:w + (hd + 1) * dk], 1.0)
        v_ref[hd] = qkv[:, 2 * w + hd * dk:2 * w + (hd + 1) * dk]
        z_ref[hd] = _silu(pd[:, 3 * w + hd * dk:3 * w + (hd + 1) * dk])
    g = -jnp.exp(dpar_ref[0:1, :]) * _softplus(pab + dpar_ref[1:2, :])
    lane = lax.broadcasted_iota(jnp.int32, pab.shape, 1)
    gb_ref[...] = jnp.where(lane < n_heads, g, _sigmoid(pab))


def _split_bf16(x, parts):
    out = []
    for _ in range(parts):
        t = x.astype(bf16)
        out.append(t)
        x = x - t.astype(f32)
    return out


def _delta_kernel(q_ref, k_ref, v_ref, z_ref, gb_ref, s0_ref, ng_ref, y_ref, so_ref, s_scr, y_scr,
                  *, n_heads, seqs_per_iter):
    c_idx = pl.program_id(1)
    _, rows, dk = q_ref.shape
    c = rows // NB
    hg = PACK // c
    n_grp = n_heads // hg
    n_pair = n_heads // 2
    ppg = hg // 2
    spi = seqs_per_iter
    assert c * hg == PACK and n_grp * hg == n_heads and 2 * ppg == hg and 2 * dk == PACK

    @pl.when(c_idx == 0)
    def _():
        s_scr[...] = jnp.zeros(s_scr.shape, f32)
        for n in range(NB):
            for h in range(n_heads):
                o = (h % 2) * dk
                s_scr[n, h // 2, o:o + dk, o:o + dk] = s0_ref[n, h]

    def iota(shape, axis):
        return lax.broadcasted_iota(jnp.int32, shape, axis)

    def blk(i, size):
        return lax.shift_right_logical(i, size.bit_length() - 1)

    row = iota((c, PACK), 0)
    lane = iota((c, PACK), 1)
    jcol = lane & (c - 1)
    incl = row >= jcol
    strict = row > jcol
    eye_f = (row == jcol).astype(f32)
    head_sel = [(blk(lane, c) == hh).astype(bf16) for hh in range(hg)]
    base = blk(row, 8) == blk(jcol, 8)
    joins = []
    s = 8
    while s < c:
        joins.append((blk(row, 2 * s) == blk(jcol, 2 * s)) & (blk(row, s) != blk(jcol, s)))
        s *= 2
    ltri = (iota((c, c), 0) >= iota((c, c), 1)).astype(bf16)
    s_diag = blk(iota((PACK, PACK), 0), dk) == blk(iota((PACK, PACK), 1), dk)
    lane128 = iota((c, LANES), 1)
    is_g = lane128 < n_heads
    heads_per_tile = LANES // c
    zero16 = jnp.zeros((c, dk), bf16)
    ng = ng_ref[...]
    nt = (((1,), (1,)), ((), ()))
    tn = (((0,), (0,)), ((), ()))
    chains = [(i, g) for i in range(spi) for g in range(n_grp)]
    pairs = [(i, p) for i in range(spi) for p in range(n_pair)]

    def bd_heads(blocks):
        return jnp.concatenate(
            [jnp.concatenate([blocks[i] if i == j else zero16 for j in range(hg)], axis=1)
             for i in range(hg)], axis=0)

    def mmp(xs, ys):
        out = []
        for x, y in zip(xs, ys):
            y16 = y.astype(bf16)
            out.append(_dot(x.astype(bf16), jnp.concatenate([y16 * m for m in head_sel], axis=0)))
        return out

    def iteration(it, carry):
        seqs = [it * spi + i for i in range(spi)]
        sls = [pl.ds(n, c, stride=NB) for n in seqs]
        sq = range(spi)

        gbs = [gb_ref[sl, :] for sl in sls]
        cs = [_dot(ltri, jnp.concatenate(_split_bf16(jnp.where(is_g, gb, 0.0), 3), axis=1)) for gb in gbs]
        gc = [t[:, 0:LANES] + t[:, LANES:2 * LANES] + t[:, 2 * LANES:3 * LANES] for t in cs]

        def per_head_lanes(a, first):
            return jnp.concatenate([jnp.broadcast_to(a[:, first + h:first + h + 1], (c, dk))
                                    for h in range(n_heads)], axis=1)

        def packed_cols(full, g):
            tiles = []
            for t in range(PACK // LANES):
                first = g * hg + t * heads_per_tile
                val = full[:, first * dk:(first + 1) * dk]
                for m in range(1, heads_per_tile):
                    val = jnp.where(lane128 >= m * c, full[:, (first + m) * dk:(first + m + 1) * dk], val)
                tiles.append(val)
            return jnp.concatenate(tiles, axis=1)

        gc_full = [per_head_lanes(t, 0) for t in gc]
        beta_full = [per_head_lanes(t, n_heads) for t in gbs]
        gcol = [packed_cols(gc_full[i], g) for i, g in chains]
        grow = [jnp.sum(t * eye_f, axis=0, keepdims=True) for t in gcol]
        decay = [jnp.exp(jnp.where(incl, gcol[j] - grow[j], -jnp.inf)) for j in range(len(chains))]
        g_last = [t[c - 1:c, :] for t in gc_full]
        e_gc = [jnp.exp(t) for t in gc_full]
        e_rest = [jnp.exp(g_last[i] - gc_full[i]) for i in sq]
        e_last = [jnp.exp(t) for t in g_last]

        def heads16(a):
            return [a[:, h * dk:(h + 1) * dk].astype(bf16) for h in range(n_heads)]

        def load(ref, sl):
            return jnp.concatenate([ref[h, sl, :] for h in range(n_heads)], axis=1)

        qh = [heads16(load(q_ref, sl)) for sl in sls]
        k32 = [load(k_ref, sl) for sl in sls]
        kh = [heads16(t) for t in k32]
        kb32 = [k32[i] * beta_full[i] for i in sq]
        kbh = [heads16(t) for t in kb32]
        kbeh = [heads16(kb32[i] * e_gc[i]) for i in sq]
        vbh = [heads16(load(v_ref, sls[i]) * beta_full[i]) for i in sq]
        kbd = [bd_heads(kh[i][g * hg:(g + 1) * hg]) for i, g in chains]
        kq = [jnp.concatenate([jnp.concatenate(kbh[i][g * hg:(g + 1) * hg], axis=1),
                               jnp.concatenate(qh[i][g * hg:(g + 1) * hg], axis=1)], axis=0)
              for i, g in chains]
        kkqk = [lax.dot_general(kq[j], kbd[j], nt, preferred_element_type=f32) for j in range(len(chains))]
        a_mat = [jnp.where(strict, kkqk[j][0:c] * decay[j], 0.0) for j in range(len(chains))]
        p_mat = [(kkqk[j][c:2 * c] * decay[j]).astype(bf16) for j in range(len(chains))]

        a0 = [jnp.where(base, m, 0.0) for m in a_mat]
        a2 = mmp(a0, a0)
        x = [eye_f - m for m in a0]
        both = mmp([jnp.concatenate([p, q], axis=0) for p, q in zip(a2, x)], a2)
        a4 = [t[0:c] for t in both]
        x = [p + t[c:2 * c] for p, t in zip(x, both)]
        x = [p + q for p, q in zip(x, mmp(x, a4))]
        for join in joins:
            low = [jnp.where(join, m, 0.0) for m in a_mat]
            x = [p - q for p, q in zip(x, mmp(mmp(x, low), x))]

        def tile_of(arrs, i, p):
            lo = (p % ppg) * 2 * c
            return arrs[i * n_grp + p // ppg][:, lo // LANES * LANES:(lo // LANES + 1) * LANES]

        def pair_rows(p, row_a, row_b):
            first = ((p % ppg) * 2 * c % LANES) // c
            zeros = [zero16] * len(row_a)
            blocks = [row_a if r == first else row_b if r == first + 1 else zeros
                      for r in range(LANES // c)]
            return jnp.concatenate([jnp.concatenate(b, axis=1) for b in blocks], axis=0)

        x16 = [t.astype(bf16) for t in x]
        uw = [_dot(tile_of(x16, i, p),
                   pair_rows(p, [vbh[i][2 * p], zero16, kbeh[i][2 * p], zero16],
                             [zero16, vbh[i][2 * p + 1], zero16, kbeh[i][2 * p + 1]]))
              for i, p in pairs]
        s_old = [s_scr[seqs[i], p] for i, p in pairs]
        s16 = [t.astype(bf16) for t in s_old]
        k_p = [jnp.concatenate(kh[i][2 * p:2 * p + 2], axis=1) for i, p in pairs]
        wq = [_dot(jnp.concatenate([uw[m][:, PACK:2 * PACK].astype(bf16),
                                    jnp.concatenate(qh[i][2 * p:2 * p + 2], axis=1)], axis=0), s16[m])
              for m, (i, p) in enumerate(pairs)]
        u = [uw[m][:, 0:PACK] - wq[m][0:c] for m in range(len(pairs))]
        u16 = [t.astype(bf16) for t in u]
        pu = [_dot(tile_of(p_mat, i, p), pair_rows(p, [u16[m][:, 0:dk], zero16], [zero16, u16[m][:, dk:PACK]]))
              for m, (i, p) in enumerate(pairs)]
        ktu = [lax.dot_general(k_p[m], (u[m] * e_rest[i][:, p * PACK:(p + 1) * PACK]).astype(bf16), tn,
                               preferred_element_type=f32) for m, (i, p) in enumerate(pairs)]
        for m, (i, p) in enumerate(pairs):
            s_scr[seqs[i], p] = (s_old[m] * e_last[i][:, p * PACK:(p + 1) * PACK]
                                 + jnp.where(s_diag, ktu[m], 0.0))
            o_pair = wq[m][c:2 * c] * e_gc[i][:, p * PACK:(p + 1) * PACK] + pu[m]
            for hh in range(2):
                h = 2 * p + hh
                o = o_pair[:, hh * dk:(hh + 1) * dk]
                ms = jnp.mean(o * o, axis=-1, keepdims=True)
                y_scr[h, sls[i], :] = o * lax.rsqrt(ms + RMS_EPS) * ng * z_ref[h, sls[i], :]
        return carry

    lax.fori_loop(0, NB // spi, iteration, 0)
    for h in range(n_heads):
        y_ref[:, h * dk:(h + 1) * dk] = y_scr[h].astype(bf16)

    @pl.when(c_idx == pl.num_programs(1) - 1)
    def _():
        for n in range(NB):
            for h in range(n_heads):
                o = (h % 2) * dk
                so_ref[n, h] = s_scr[n, h // 2, o:o + dk, o:o + dk]


def _mix_out_kernel(x_ref, xb_ref, ya_ref, yb_ref, yd_ref, wc_ref, stc_ref, ccw_ref, ccb_ref, cg_ref,
                    cbeta_ref, wg_ref, bg_ref, wbr_ref, wo_ref, g_ref, b_ref,
                    xo_ref, xbo_ref, stoc_ref, cbc, *, alpha):
    t_idx = pl.program_id(1)
    rows, d = x_ref.shape
    w = ya_ref.shape[1]
    _halo_init(cbc, stc_ref, t_idx)
    xb = xb_ref[...]

    pc = _wdot(xb, wc_ref[...])
    glu = pc[:, 0:w] * _sigmoid(pc[:, w:2 * w])
    conv = _causal_conv(cbc, glu, ccw_ref[...], stc_ref.shape[0]) + ccb_ref[...]
    _halo_roll(cbc, stoc_ref, rows)
    y_c = (_silu(_layer_norm(conv, cg_ref[...], cbeta_ref[...])) * _silu(pc[:, 2 * w:3 * w])).astype(bf16)

    gates = _sigmoid(_wdot(xb, wg_ref[...]) + bg_ref[...])
    mixed = None
    for n, y in enumerate((ya_ref[...], yb_ref[...], y_c, yd_ref[...])):
        t = gates[:, n * d:(n + 1) * d] * _wdot(y, wbr_ref[n])
        mixed = t if mixed is None else mixed + t
    out = _wdot(mixed.astype(bf16), wo_ref[...])
    xn = _layer_norm(alpha * x_ref[...] + out, g_ref[...], b_ref[...])
    xo_ref[...] = xn
    xbo_ref[...] = xn.astype(bf16)


def _tile_spec(rows, width, n_t):
    return pl.BlockSpec((rows, width), lambda g, t: (g * n_t + t, 0))


def _group_spec(shape):
    n = len(shape)
    return pl.BlockSpec((None,) + tuple(shape), lambda g, t: (g,) + (0,) * n)


def _const_spec(shape):
    n = len(shape)
    return pl.BlockSpec(tuple(shape), lambda g, t: (0,) * n, pipeline_mode=pl.Buffered(1))


def _layer(x, xb, st, p, *, n_groups, seq_len, tt, alpha):
    conv_a, conv_b, lru, conv_c, conv_d, delta = st
    t_rows, d = x.shape
    w = p["a_conv_w"].shape[1]
    n_t = seq_len // tt
    rows = tt * NB
    grid = (n_groups, n_t)
    tile_in = _tile_spec(rows, d, n_t)
    tile_w = _tile_spec(rows, w, n_t)
    tile_gb = _tile_spec(rows, LANES, n_t)
    act16 = jax.ShapeDtypeStruct((t_rows, w), bf16)

    def shape_of(a):
        return jax.ShapeDtypeStruct(a.shape, f32)

    def grp(a):
        return _group_spec(a.shape[1:])

    def const(name):
        return _const_spec(p[name].shape)

    def cbuf(a):
        return pltpu.VMEM((a.shape[1] + rows, a.shape[2]), f32)

    n_heads, dk, dv = delta.shape[1:]
    assert dk == LANES and dv == LANES
    head_rows = jax.ShapeDtypeStruct((n_heads, t_rows, dk), f32)
    head_tile = pl.BlockSpec((n_heads, rows, dk), lambda g, t: (0, g * n_t + t, 0))
    in_names = ("w_a", "w_b", "w_d", "w_dab")
    par_names = ("a_conv_w", "b_conv_w", "b_conv_b", "b_wxa", "b_bx", "b_ba", "b_lambda", "d_conv_w",
                 "d_par")
    y_a, y_b, q, k, v, z, gb, new_a, new_b, new_lru, new_d = pl.pallas_call(
        _mix_in_kernel,
        grid=grid,
        in_specs=[tile_in] + [const(n) for n in in_names] + [grp(conv_a), grp(conv_b), grp(lru), grp(conv_d)]
        + [const(n) for n in par_names],
        out_specs=[tile_w, tile_w] + [head_tile] * 4 + [tile_gb, grp(conv_a), grp(conv_b), grp(lru),
                                                        grp(conv_d)],
        out_shape=[act16, act16] + [head_rows] * 4 + [jax.ShapeDtypeStruct((t_rows, LANES), f32),
                                                      shape_of(conv_a), shape_of(conv_b), shape_of(lru),
                                                      shape_of(conv_d)],
        scratch_shapes=[cbuf(conv_a), cbuf(conv_b), cbuf(conv_d), pltpu.VMEM(lru.shape[1:], f32)],
        compiler_params=_params(2),
        name="mix_in",
    )(xb, *[p[n] for n in in_names], conv_a, conv_b, lru, conv_d, *[p[n] for n in par_names])

    chunk = min(GDN_CHUNK, seq_len)
    n_c = seq_len // chunk
    c_rows = chunk * NB
    head_spec = pl.BlockSpec((n_heads, c_rows, dk), lambda g, c: (0, g * n_c + c, 0))
    state_spec = pl.BlockSpec((NB, n_heads, dk, dv), lambda g, c: (g, 0, 0, 0))
    y_d, new_delta = pl.pallas_call(
        functools.partial(_delta_kernel, n_heads=n_heads, seqs_per_iter=DELTA_SEQS),
        grid=(n_groups, n_c),
        in_specs=[head_spec, head_spec, head_spec, head_spec, _tile_spec(c_rows, LANES, n_c),
                  state_spec, const("d_norm_g")],
        out_specs=[_tile_spec(c_rows, w, n_c), state_spec],
        out_shape=[act16, shape_of(delta)],
        scratch_shapes=[pltpu.VMEM((NB, n_heads // 2, PACK, PACK), f32),
                        pltpu.VMEM((n_heads, c_rows, dk), f32)],
        compiler_params=_params(2),
        name="delta",
    )(q, k, v, z, gb, delta, p["d_norm_g"])

    out_names = ("c_conv_w", "c_conv_b", "c_ln_g", "c_ln_b", "w_g", "b_gate", "w_branch", "w_out",
                 "ln_g", "ln_b")
    x_new, xb_new, new_c = pl.pallas_call(
        functools.partial(_mix_out_kernel, alpha=alpha),
        grid=grid,
        in_specs=[tile_in, tile_in, tile_w, tile_w, tile_w, const("w_c"), grp(conv_c)]
        + [const(n) for n in out_names],
        out_specs=[tile_in, tile_in, grp(conv_c)],
        out_shape=[jax.ShapeDtypeStruct((t_rows, d), f32), jax.ShapeDtypeStruct((t_rows, d), bf16),
                   shape_of(conv_c)],
        scratch_shapes=[cbuf(conv_c)],
        compiler_params=_params(2),
        name="mix_out",
    )(x, xb, y_a, y_b, y_d, p["w_c"], conv_c, *[p[n] for n in out_names])

    return x_new, xb_new, (new_a, new_b, new_lru, new_c, new_d, new_delta)


def _rows_from_seq(x):
    b, l, d = x.shape
    return x.reshape(b // NB, NB, l, d).transpose(0, 2, 1, 3).reshape(b * l, d)


def _seq_from_rows(x, b, l):
    d = x.shape[-1]
    return x.reshape(b // NB, l, NB, d).transpose(0, 2, 1, 3).reshape(b, l, d)


def _conv_state_to_rows(s):
    b, k1, c = s.shape
    return s.reshape(b // NB, NB, k1, c).transpose(0, 2, 1, 3).reshape(b // NB, k1 * NB, c)


def _conv_state_from_rows(s, b):
    g, r, c = s.shape
    k1 = r // NB
    return s.reshape(g, k1, NB, c).transpose(0, 2, 1, 3).reshape(b, k1, c)


def _run_trunk(x, states, ln_in_g, ln_in_b, params, depth, alpha):
    b, l, d = x.shape
    assert b % NB == 0
    n_groups = b // NB
    tt = min(l, TILE_FRAMES)
    assert l % tt == 0 and l % min(GDN_CHUNK, l) == 0
    conv_a, conv_b, lru, conv_c, conv_d, delta = states
    rows = _rows_from_seq(x)
    xc, xbc = _ln_in(rows, ln_in_g, ln_in_b, tt * NB)
    new = []
    for i in range(depth):
        st = (_conv_state_to_rows(conv_a[i]), _conv_state_to_rows(conv_b[i]),
              lru[i].reshape(n_groups, NB, -1), _conv_state_to_rows(conv_c[i]),
              _conv_state_to_rows(conv_d[i]), delta[i])
        p = {name: arr[i] for name, arr in params.items()}
        xc, xbc, (na, nb_, nl, nc, nd, ndelta) = _layer(xc, xbc, st, p, n_groups=n_groups, seq_len=l,
                                                        tt=tt, alpha=alpha)
        new.append((_conv_state_from_rows(na, b), _conv_state_from_rows(nb_, b), nl.reshape(b, -1),
                    _conv_state_from_rows(nc, b), _conv_state_from_rows(nd, b), ndelta))
    return _seq_from_rows(xc, b, l), tuple(jnp.stack(s) for s in zip(*new))


def kernel(x_prompt, x_sample, state_conv_a, state_conv_b, state_lru, state_conv_c, state_conv_d, state_delta, ln_in_g, ln_in_b, w_in, b_gate, a_conv_w, b_conv_w, b_conv_b, b_wx, b_bx, b_wa, b_ba, b_lambda, c_conv_w, c_conv_b, c_ln_g, c_ln_b, d_conv_w, d_a_log, d_dt_bias, d_norm_g, w_branch, w_out, ln_g, ln_b):
    depth, d, _ = w_in.shape
    w_a, w_b, w_c = a_conv_w.shape[2], b_conv_w.shape[2], c_conv_w.shape[2]
    w_qkv = d_conv_w.shape[2]
    n_heads, dv = d_a_log.shape[1], d_norm_g.shape[1]
    w_d = n_heads * dv
    assert n_heads <= LANES // 2
    off_b = 4 * w_a
    off_c = off_b + 2 * w_b
    off_d = off_c + 3 * w_c
    off_da = off_d + w_qkv + w_d
    off_g = off_da + 2 * n_heads
    alpha = (2 * depth) ** 0.25

    def row(a):
        return a.reshape(depth, 1, -1)

    def cols(lo, hi):
        return _pack_rows(w_in[:, :, lo:hi])

    d_par = jnp.zeros((depth, 8, LANES), f32)
    d_par = d_par.at[:, 0, :n_heads].set(d_a_log).at[:, 1, :n_heads].set(d_dt_bias)
    params = dict(
        w_a=cols(0, off_b), w_b=cols(off_b, off_c), w_c=cols(off_c, off_d),
        w_d=cols(off_d, off_da),
        w_dab=jnp.pad(w_in[:, :, off_da:off_g], ((0, 0), (0, 0), (0, LANES - 2 * n_heads))).astype(bf16),
        w_g=cols(off_g, off_g + b_gate.shape[1] * d),
        b_gate=row(b_gate), a_conv_w=a_conv_w, b_conv_w=b_conv_w, b_conv_b=row(b_conv_b),
        b_wxa=jnp.concatenate([b_wx, b_wa], axis=-1).astype(bf16),
        b_bx=row(b_bx), b_ba=row(b_ba), b_lambda=row(b_lambda),
        c_conv_w=c_conv_w, c_conv_b=row(c_conv_b), c_ln_g=row(c_ln_g), c_ln_b=row(c_ln_b),
        d_conv_w=d_conv_w, d_par=d_par, d_norm_g=row(d_norm_g),
        w_branch=_pack_rows(w_branch.reshape((-1,) + w_branch.shape[2:])).reshape(
            w_branch.shape[:2] + (w_branch.shape[2] // 2, w_branch.shape[3])),
        w_out=_pack_rows(w_out), ln_g=row(ln_g), ln_b=row(ln_b),
    )
    g_in, b_in = ln_in_g.reshape(1, d), ln_in_b.reshape(1, d)

    bp = x_prompt.shape[0]

    def zeros(s):
        return jnp.zeros((depth, bp) + s.shape[2:], s.dtype)

    prompt_states = tuple(zeros(s) for s in (state_conv_a, state_conv_b, state_lru, state_conv_c,
                                              state_conv_d, state_delta))
    y_prompt, p_st = _run_trunk(x_prompt, prompt_states, g_in, b_in, params, depth, alpha)
    sample_states = (state_conv_a, state_conv_b, state_lru, state_conv_c, state_conv_d, state_delta)
    y_sample, s_st = _run_trunk(x_sample, sample_states, g_in, b_in, params, depth, alpha)
    return (y_prompt, y_sample) + tuple(p_st) + tuple(s_st)
```

```python
import functools

import jax
import jax.numpy as jnp
from jax import lax
from jax.experimental import pallas as pl
from jax.experimental.pallas import tpu as pltpu

f32 = jnp.float32
bf16 = jnp.bfloat16

LN_EPS = 1e-5
RMS_EPS = 1e-6
L2_EPS = 1e-6
LRU_C = 8.0
GDN_CHUNK = 64
NB = 8
LANES = 128
PACK = 2 * LANES
DELTA_SEQS = 4
TILE_FRAMES = 32
VMEM_LIMIT = 56 * 1024 * 1024


def _sigmoid(x):
    return 0.5 * jnp.tanh(0.5 * x) + 0.5


def _silu(x):
    h = 0.5 * x
    return h * jnp.tanh(h) + h


def _softplus(x):
    return jnp.maximum(x, 0.0) + jnp.log1p(jnp.exp(-jnp.abs(x)))


def _layer_norm(x, g, b):
    mu = jnp.mean(x, axis=-1, keepdims=True)
    xc = x - mu
    var = jnp.mean(xc * xc, axis=-1, keepdims=True)
    return xc * lax.rsqrt(var + LN_EPS) * g + b


def _dot(a, b):
    return jnp.dot(a, b, preferred_element_type=f32)


def _params(n_axes):
    return pltpu.CompilerParams(dimension_semantics=("arbitrary",) * n_axes,
                                vmem_limit_bytes=VMEM_LIMIT)


PACK_ROWS = 256


def _pack_rows_kernel(w_ref, o_ref):
    o_ref[...] = pltpu.bitcast(w_ref[...].astype(bf16), jnp.uint32)


def _pack_rows(w, lo=0, width=None):
    lead, k, n = w.shape
    width = n if width is None else width
    assert lo % width == 0 and lo + width <= n
    col = lo // width
    return pl.pallas_call(
        _pack_rows_kernel,
        grid=(lead, k // PACK_ROWS),
        in_specs=[pl.BlockSpec((None, PACK_ROWS, width), lambda l, i: (l, i, col))],
        out_specs=pl.BlockSpec((None, PACK_ROWS // 2, width), lambda l, i: (l, i, 0)),
        out_shape=jax.ShapeDtypeStruct((lead, k // 2, width), jnp.uint32),
        compiler_params=_params(2),
        name="pack_rows",
    )(w)


def _wdot(a, w_packed):
    return _dot(a, pltpu.bitcast(w_packed, bf16))


def _halo_init(cbuf, st_ref, t_idx):
    @pl.when(t_idx == 0)
    def _():
        cbuf[0:st_ref.shape[0], :] = st_ref[...]


def _causal_conv(cbuf, u, cw, halo):
    rows = u.shape[0]
    cbuf[halo:halo + rows, :] = u
    acc = None
    for k in range(cw.shape[0]):
        t = cbuf[k * NB:k * NB + rows, :] * cw[k:k + 1, :]
        acc = t if acc is None else acc + t
    return acc


def _halo_roll(cbuf, sto_ref, rows):
    halo = sto_ref.shape[0]
    new = cbuf[rows:rows + halo, :]
    cbuf[0:halo, :] = new
    sto_ref[...] = new


def _ln_in_kernel(x_ref, g_ref, b_ref, xo_ref, xb_ref):
    y = _layer_norm(x_ref[...], g_ref[...], b_ref[...])
    xo_ref[...] = y
    xb_ref[...] = y.astype(bf16)


def _ln_in(x, g, b, rows):
    t, d = x.shape
    tile = pl.BlockSpec((rows, d), lambda i: (i, 0))
    vec = pl.BlockSpec((1, d), lambda i: (0, 0))
    return pl.pallas_call(
        _ln_in_kernel,
        grid=(t // rows,),
        in_specs=[tile, vec, vec],
        out_specs=[tile, tile],
        out_shape=[jax.ShapeDtypeStruct((t, d), f32), jax.ShapeDtypeStruct((t, d), bf16)],
        compiler_params=_params(1),
        name="ln_in",
    )(x, g, b)


def _mix_in_kernel(xb_ref, wa_ref, wb_ref, wd_ref, wdab_ref, sta_ref, stb_ref, h0_ref, std_ref,
                   acw_ref, bcw_ref, bcb_ref, wxa_ref, bx_ref, ba_ref, lam_ref, dcw_ref, dpar_ref,
                   ya_ref, yb_ref, q_ref, k_ref, v_ref, z_ref, gb_ref,
                   stoa_ref, stob_ref, ho_ref, stod_ref,
                   cba, cbb, cbd, h_scr):
    t_idx = pl.program_id(1)
    rows, w = ya_ref.shape
    n_heads, _, dk = q_ref.shape
    n_blk, bw, _ = wxa_ref.shape
    _halo_init(cba, sta_ref, t_idx)
    _halo_init(cbb, stb_ref, t_idx)
    _halo_init(cbd, std_ref, t_idx)

    @pl.when(t_idx == 0)
    def _():
        h_scr[...] = h0_ref[...]

    xb = xb_ref[...]

    pa = _wdot(xb, wa_ref[...])
    a_b, a_c, a_x, a_z = (pa[:, n * w:(n + 1) * w] for n in range(4))
    conv = _causal_conv(cba, a_c * a_x, acw_ref[...], sta_ref.shape[0])
    ya_ref[...] = (a_b * conv * _silu(a_z)).astype(bf16)
    _halo_roll(cba, stoa_ref, rows)

    pb = _wdot(xb, wb_ref[...])
    xc = _causal_conv(cbb, pb[:, 0:w], bcw_ref[...], stb_ref.shape[0]) + bcb_ref[...]
    _halo_roll(cbb, stob_ref, rows)
    xc16 = xc.astype(bf16)
    pre = [_dot(xc16[:, h * bw:(h + 1) * bw], wxa_ref[h]) for h in range(n_blk)]
    gate_x = _sigmoid(jnp.concatenate([g[:, 0:bw] for g in pre], axis=1) + bx_ref[...])
    gate_a = _sigmoid(jnp.concatenate([g[:, bw:2 * bw] for g in pre], axis=1) + ba_ref[...])
    a = jnp.exp(gate_a * (-LRU_C * _softplus(-lam_ref[...])))
    bterm = jnp.sqrt(1.0 - a * a) * gate_x * xc
    h = h_scr[...]
    hs = []
    for s in range(rows // NB):
        h = a[s * NB:(s + 1) * NB, :] * h + bterm[s * NB:(s + 1) * NB, :]
        hs.append(h)
    h_scr[...] = h
    ho_ref[...] = h
    yb_ref[...] = (jnp.concatenate(hs, axis=0) * _silu(pb[:, w:2 * w])).astype(bf16)

    pd = _wdot(xb, wd_ref[...])
    pab = _dot(xb, wdab_ref[...])
    qkv = _silu(_causal_conv(cbd, pd[:, 0:3 * w], dcw_ref[...], std_ref.shape[0]))
    _halo_roll(cbd, stod_ref, rows)

    def l2n(xh, scale):
        ss = jnp.sum(xh * xh, axis=-1, keepdims=True)
        return xh * (lax.rsqrt(ss + L2_EPS) * scale)

    for hd in range(n_heads):
        q_ref[hd] = l2n(qkv[:, hd * dk:(hd + 1) * dk], dk ** -0.5)
        k_ref[hd] = l2n(qkv[:, w + hd * dk:w + (hd + 1) * dk], 1.0)
        v_ref[hd] = qkv[:, 2 * w + hd * dk:2 * w + (hd + 1) * dk]
        z_ref[hd] = _silu(pd[:, 3 * w + hd * dk:3 * w + (hd + 1) * dk])
    g = -jnp.exp(dpar_ref[0:1, :]) * _softplus(pab + dpar_ref[1:2, :])
    lane = lax.broadcasted_iota(jnp.int32, pab.shape, 1)
    gb_ref[...] = jnp.where(lane < n_heads, g, _sigmoid(pab))


def _split_bf16(x, parts):
    out = []
    for _ in range(parts):
        t = x.astype(bf16)
        out.append(t)
        x = x - t.astype(f32)
    return out


def _delta_kernel(q_ref, k_ref, v_ref, z_ref, gb_ref, s0_ref, ng_ref, y_ref, so_ref, s_scr, y_scr,
                  *, n_heads, seqs_per_iter):
    c_idx = pl.program_id(1)
    _, rows, dk = q_ref.shape
    c = rows // NB
    hg = PACK // c
    n_grp = n_heads // hg
    n_pair = n_heads // 2
    ppg = hg // 2
    spi = seqs_per_iter
    assert c * hg == PACK and n_grp * hg == n_heads and 2 * ppg == hg and 2 * dk == PACK

    @pl.when(c_idx == 0)
    def _():
        s_scr[...] = jnp.zeros(s_scr.shape, f32)
        for n in range(NB):
            for h in range(n_heads):
                o = (h % 2) * dk
                s_scr[n, h // 2, o:o + dk, o:o + dk] = s0_ref[n, h]

    def iota(shape, axis):
        return lax.broadcasted_iota(jnp.int32, shape, axis)

    def blk(i, size):
        return lax.shift_right_logical(i, size.bit_length() - 1)

    row = iota((c, PACK), 0)
    lane = iota((c, PACK), 1)
    jcol = lane & (c - 1)
    incl = row >= jcol
    strict = row > jcol
    eye_f = (row == jcol).astype(f32)
    head_sel = [(blk(lane, c) == hh).astype(bf16) for hh in range(hg)]
    base = blk(row, 8) == blk(jcol, 8)
    joins = []
    s = 8
    while s < c:
        joins.append((blk(row, 2 * s) == blk(jcol, 2 * s)) & (blk(row, s) != blk(jcol, s)))
        s *= 2
    ltri = (iota((c, c), 0) >= iota((c, c), 1)).astype(bf16)
    s_diag = blk(iota((PACK, PACK), 0), dk) == blk(iota((PACK, PACK), 1), dk)
    lane128 = iota((c, LANES), 1)
    is_g = lane128 < n_heads
    heads_per_tile = LANES // c
    zero16 = jnp.zeros((c, dk), bf16)
    ng = ng_ref[...]
    nt = (((1,), (1,)), ((), ()))
    tn = (((0,), (0,)), ((), ()))
    chains = [(i, g) for i in range(spi) for g in range(n_grp)]
    pairs = [(i, p) for i in range(spi) for p in range(n_pair)]

    def bd_heads(blocks):
        return jnp.concatenate(
            [jnp.concatenate([blocks[i] if i == j else zero16 for j in range(hg)], axis=1)
             for i in range(hg)], axis=0)

    def mmp(xs, ys):
        out = []
        for x, y in zip(xs, ys):
            y16 = y.astype(bf16)
            out.append(_dot(x.astype(bf16), jnp.concatenate([y16 * m for m in head_sel], axis=0)))
        return out

    def iteration(it, carry):
        seqs = [it * spi + i for i in range(spi)]
        sls = [pl.ds(n, c, stride=NB) for n in seqs]
        sq = range(spi)

        gbs = [gb_ref[sl, :] for sl in sls]
        cs = [_dot(ltri, jnp.concatenate(_split_bf16(jnp.where(is_g, gb, 0.0), 3), axis=1)) for gb in gbs]
        gc = [t[:, 0:LANES] + t[:, LANES:2 * LANES] + t[:, 2 * LANES:3 * LANES] for t in cs]

        def per_head_lanes(a, first):
            return jnp.concatenate([jnp.broadcast_to(a[:, first + h:first + h + 1], (c, dk))
                                    for h in range(n_heads)], axis=1)

        def packed_cols(full, g):
            tiles = []
            for t in range(PACK // LANES):
                first = g * hg + t * heads_per_tile
                val = full[:, first * dk:(first + 1) * dk]
                for m in range(1, heads_per_tile):
                    val = jnp.where(lane128 >= m * c, full[:, (first + m) * dk:(first + m + 1) * dk], val)
                tiles.append(val)
            return jnp.concatenate(tiles, axis=1)

        gc_full = [per_head_lanes(t, 0) for t in gc]
        beta_full = [per_head_lanes(t, n_heads) for t in gbs]
        gcol = [packed_cols(gc_full[i], g) for i, g in chains]
        grow = [jnp.sum(t * eye_f, axis=0, keepdims=True) for t in gcol]
        decay = [jnp.exp(jnp.where(incl, gcol[j] - grow[j], -jnp.inf)) for j in range(len(chains))]
        g_last = [t[c - 1:c, :] for t in gc_full]
        e_gc = [jnp.exp(t) for t in gc_full]
        e_rest = [jnp.exp(g_last[i] - gc_full[i]) for i in sq]
        e_last = [jnp.exp(t) for t in g_last]

        def heads16(a):
            return [a[:, h * dk:(h + 1) * dk].astype(bf16) for h in range(n_heads)]

        def load(ref, sl):
            return jnp.concatenate([ref[h, sl, :] for h in range(n_heads)], axis=1)

        qh = [heads16(load(q_ref, sl)) for sl in sls]
        k32 = [load(k_ref, sl) for sl in sls]
        kh = [heads16(t) for t in k32]
        kb32 = [k32[i] * beta_full[i] for i in sq]
        kbh = [heads16(t) for t in kb32]
        kbeh = [heads16(kb32[i] * e_gc[i]) for i in sq]
        vbh = [heads16(load(v_ref, sls[i]) * beta_full[i]) for i in sq]
        kbd = [bd_heads(kh[i][g * hg:(g + 1) * hg]) for i, g in chains]
        kq = [jnp.concatenate([jnp.concatenate(kbh[i][g * hg:(g + 1) * hg], axis=1),
                               jnp.concatenate(qh[i][g * hg:(g + 1) * hg], axis=1)], axis=0)
              for i, g in chains]
        kkqk = [lax.dot_general(kq[j], kbd[j], nt, preferred_element_type=f32) for j in range(len(chains))]
        a_mat = [jnp.where(strict, kkqk[j][0:c] * decay[j], 0.0) for j in range(len(chains))]
        p_mat = [(kkqk[j][c:2 * c] * decay[j]).astype(bf16) for j in range(len(chains))]

        a0 = [jnp.where(base, m, 0.0) for m in a_mat]
        a2 = mmp(a0, a0)
        x = [eye_f - m for m in a0]
        both = mmp([jnp.concatenate([p, q], axis=0) for p, q in zip(a2, x)], a2)
        a4 = [t[0:c] for t in both]
        x = [p + t[c:2 * c] for p, t in zip(x, both)]
        x = [p + q for p, q in zip(x, mmp(x, a4))]
        for join in joins:
            low = [jnp.where(join, m, 0.0) for m in a_mat]
            x = [p - q for p, q in zip(x, mmp(mmp(x, low), x))]

        def tile_of(arrs, i, p):
            lo = (p % ppg) * 2 * c
            return arrs[i * n_grp + p // ppg][:, lo // LANES * LANES:(lo // LANES + 1) * LANES]

        def pair_rows(p, row_a, row_b):
            first = ((p % ppg) * 2 * c % LANES) // c
            zeros = [zero16] * len(row_a)
            blocks = [row_a if r == first else row_b if r == first + 1 else zeros
                      for r in range(LANES // c)]
            return jnp.concatenate([jnp.concatenate(b, axis=1) for b in blocks], axis=0)

        x16 = [t.astype(bf16) for t in x]
        uw = [_dot(tile_of(x16, i, p),
                   pair_rows(p, [vbh[i][2 * p], zero16, kbeh[i][2 * p], zero16],
                             [zero16, vbh[i][2 * p + 1], zero16, kbeh[i][2 * p + 1]]))
              for i, p in pairs]
        s_old = [s_scr[seqs[i], p] for i, p in pairs]
        s16 = [t.astype(bf16) for t in s_old]
        k_p = [jnp.concatenate(kh[i][2 * p:2 * p + 2], axis=1) for i, p in pairs]
        wq = [_dot(jnp.concatenate([uw[m][:, PACK:2 * PACK].astype(bf16),
                                    jnp.concatenate(qh[i][2 * p:2 * p + 2], axis=1)], axis=0), s16[m])
              for m, (i, p) in enumerate(pairs)]
        u = [uw[m][:, 0:PACK] - wq[m][0:c] for m in range(len(pairs))]
        u16 = [t.astype(bf16) for t in u]
        pu = [_dot(tile_of(p_mat, i, p), pair_rows(p, [u16[m][:, 0:dk], zero16], [zero16, u16[m][:, dk:PACK]]))
              for m, (i, p) in enumerate(pairs)]
        ktu = [lax.dot_general(k_p[m], (u[m] * e_rest[i][:, p * PACK:(p + 1) * PACK]).astype(bf16), tn,
                               preferred_element_type=f32) for m, (i, p) in enumerate(pairs)]
        for m, (i, p) in enumerate(pairs):
            s_scr[seqs[i], p] = (s_old[m] * e_last[i][:, p * PACK:(p + 1) * PACK]
                                 + jnp.where(s_diag, ktu[m], 0.0))
            o_pair = wq[m][c:2 * c] * e_gc[i][:, p * PACK:(p + 1) * PACK] + pu[m]
            for hh in range(2):
                h = 2 * p + hh
                o = o_pair[:, hh * dk:(hh + 1) * dk]
                ms = jnp.mean(o * o, axis=-1, keepdims=True)
                y_scr[h, sls[i], :] = o * lax.rsqrt(ms + RMS_EPS) * ng * z_ref[h, sls[i], :]
        return carry

    lax.fori_loop(0, NB // spi, iteration, 0)
    for h in range(n_heads):
        y_ref[:, h * dk:(h + 1) * dk] = y_scr[h].astype(bf16)

    @pl.when(c_idx == pl.num_programs(1) - 1)
    def _():
        for n in range(NB):
            for h in range(n_heads):
                o = (h % 2) * dk
                so_ref[n, h] = s_scr[n, h // 2, o:o + dk, o:o + dk]


def _mix_out_kernel(x_ref, xb_ref, ya_ref, yb_ref, yd_ref, wc_ref, stc_ref, ccw_ref, ccb_ref, cg_ref,
                    cbeta_ref, wg_ref, bg_ref, wbr_ref, wo_ref, g_ref, b_ref,
                    xo_ref, xbo_ref, stoc_ref, cbc, *, alpha):
    t_idx = pl.program_id(1)
    rows, d = x_ref.shape
    w = ya_ref.shape[1]
    _halo_init(cbc, stc_ref, t_idx)
    xb = xb_ref[...]

    pc = _wdot(xb, wc_ref[...])
    glu = pc[:, 0:w] * _sigmoid(pc[:, w:2 * w])
    conv = _causal_conv(cbc, glu, ccw_ref[...], stc_ref.shape[0]) + ccb_ref[...]
    _halo_roll(cbc, stoc_ref, rows)
    y_c = (_silu(_layer_norm(conv, cg_ref[...], cbeta_ref[...])) * _silu(pc[:, 2 * w:3 * w])).astype(bf16)

    gates = _sigmoid(_wdot(xb, wg_ref[...]) + bg_ref[...])
    mixed = None
    for n, y in enumerate((ya_ref[...], yb_ref[...], y_c, yd_ref[...])):
        t = gates[:, n * d:(n + 1) * d] * _wdot(y, wbr_ref[n])
        mixed = t if mixed is None else mixed + t
    out = _wdot(mixed.astype(bf16), wo_ref[...])
    xn = _layer_norm(alpha * x_ref[...] + out, g_ref[...], b_ref[...])
    xo_ref[...] = xn
    xbo_ref[...] = xn.astype(bf16)


def _tile_spec(rows, width, n_t):
    return pl.BlockSpec((rows, width), lambda g, t: (g * n_t + t, 0))


def _group_spec(shape):
    n = len(shape)
    return pl.BlockSpec((None,) + tuple(shape), lambda g, t: (g,) + (0,) * n)


def _const_spec(shape):
    n = len(shape)
    return pl.BlockSpec(tuple(shape), lambda g, t: (0,) * n, pipeline_mode=pl.Buffered(1))


def _layer(x, xb, st, p, *, n_groups, seq_len, tt, alpha):
    conv_a, conv_b, lru, conv_c, conv_d, delta = st
    t_rows, d = x.shape
    w = p["a_conv_w"].shape[1]
    n_t = seq_len // tt
    rows = tt * NB
    grid = (n_groups, n_t)
    tile_in = _tile_spec(rows, d, n_t)
    tile_w = _tile_spec(rows, w, n_t)
    tile_gb = _tile_spec(rows, LANES, n_t)
    act16 = jax.ShapeDtypeStruct((t_rows, w), bf16)

    def shape_of(a):
        return jax.ShapeDtypeStruct(a.shape, f32)

    def grp(a):
        return _group_spec(a.shape[1:])

    def const(name):
        return _const_spec(p[name].shape)

    def cbuf(a):
        return pltpu.VMEM((a.shape[1] + rows, a.shape[2]), f32)

    n_heads, dk, dv = delta.shape[1:]
    assert dk == LANES and dv == LANES
    head_rows = jax.ShapeDtypeStruct((n_heads, t_rows, dk), f32)
    head_tile = pl.BlockSpec((n_heads, rows, dk), lambda g, t: (0, g * n_t + t, 0))
    in_names = ("w_a", "w_b", "w_d", "w_dab")
    par_names = ("a_conv_w", "b_conv_w", "b_conv_b", "b_wxa", "b_bx", "b_ba", "b_lambda", "d_conv_w",
                 "d_par")
    y_a, y_b, q, k, v, z, gb, new_a, new_b, new_lru, new_d = pl.pallas_call(
        _mix_in_kernel,
        grid=grid,
        in_specs=[tile_in] + [const(n) for n in in_names] + [grp(conv_a), grp(conv_b), grp(lru), grp(conv_d)]
        + [const(n) for n in par_names],
        out_specs=[tile_w, tile_w] + [head_tile] * 4 + [tile_gb, grp(conv_a), grp(conv_b), grp(lru),
                                                        grp(conv_d)],
        out_shape=[act16, act16] + [head_rows] * 4 + [jax.ShapeDtypeStruct((t_rows, LANES), f32),
                                                      shape_of(conv_a), shape_of(conv_b), shape_of(lru),
                                                      shape_of(conv_d)],
        scratch_shapes=[cbuf(conv_a), cbuf(conv_b), cbuf(conv_d), pltpu.VMEM(lru.shape[1:], f32)],
        compiler_params=_params(2),
        name="mix_in",
    )(xb, *[p[n] for n in in_names], conv_a, conv_b, lru, conv_d, *[p[n] for n in par_names])

    chunk = min(GDN_CHUNK, seq_len)
    n_c = seq_len // chunk
    c_rows = chunk * NB
    head_spec = pl.BlockSpec((n_heads, c_rows, dk), lambda g, c: (0, g * n_c + c, 0))
    state_spec = pl.BlockSpec((NB, n_heads, dk, dv), lambda g, c: (g, 0, 0, 0))
    y_d, new_delta = pl.pallas_call(
        functools.partial(_delta_kernel, n_heads=n_heads, seqs_per_iter=DELTA_SEQS),
        grid=(n_groups, n_c),
        in_specs=[head_spec, head_spec, head_spec, head_spec, _tile_spec(c_rows, LANES, n_c),
                  state_spec, const("d_norm_g")],
        out_specs=[_tile_spec(c_rows, w, n_c), state_spec],
        out_shape=[act16, shape_of(delta)],
        scratch_shapes=[pltpu.VMEM((NB, n_heads // 2, PACK, PACK), f32),
                        pltpu.VMEM((n_heads, c_rows, dk), f32)],
        compiler_params=_params(2),
        name="delta",
    )(q, k, v, z, gb, delta, p["d_norm_g"])

    out_names = ("c_conv_w", "c_conv_b", "c_ln_g", "c_ln_b", "w_g", "b_gate", "w_branch", "w_out",
                 "ln_g", "ln_b")
    x_new, xb_new, new_c = pl.pallas_call(
        functools.partial(_mix_out_kernel, alpha=alpha),
        grid=grid,
        in_specs=[tile_in, tile_in, tile_w, tile_w, tile_w, const("w_c"), grp(conv_c)]
        + [const(n) for n in out_names],
        out_specs=[tile_in, tile_in, grp(conv_c)],
        out_shape=[jax.ShapeDtypeStruct((t_rows, d), f32), jax.ShapeDtypeStruct((t_rows, d), bf16),
                   shape_of(conv_c)],
        scratch_shapes=[cbuf(conv_c)],
        compiler_params=_params(2),
        name="mix_out",
    )(x, xb, y_a, y_b, y_d, p["w_c"], conv_c, *[p[n] for n in out_names])

    return x_new, xb_new, (new_a, new_b, new_lru, new_c, new_d, new_delta)


def _rows_from_seq(x):
    b, l, d = x.shape
    return x.reshape(b // NB, NB, l, d).transpose(0, 2, 1, 3).reshape(b * l, d)


def _seq_from_rows(x, b, l):
    d = x.shape[-1]
    return x.reshape(b // NB, l, NB, d).transpose(0, 2, 1, 3).reshape(b, l, d)


def _conv_state_to_rows(s):
    b, k1, c = s.shape
    return s.reshape(b // NB, NB, k1, c).transpose(0, 2, 1, 3).reshape(b // NB, k1 * NB, c)


def _conv_state_from_rows(s, b):
    g, r, c = s.shape
    k1 = r // NB
    return s.reshape(g, k1, NB, c).transpose(0, 2, 1, 3).reshape(b, k1, c)


def _run_trunk(x, states, ln_in_g, ln_in_b, params, depth, alpha):
    b, l, d = x.shape
    assert b % NB == 0
    n_groups = b // NB
    tt = min(l, TILE_FRAMES)
    assert l % tt == 0 and l % min(GDN_CHUNK, l) == 0
    conv_a, conv_b, lru, conv_c, conv_d, delta = states
    rows = _rows_from_seq(x)
    ln_rows = tt * NB * (4 if (b * l) % (4 * tt * NB) == 0 else 1)
    xc, xbc = _ln_in(rows, ln_in_g, ln_in_b, ln_rows)
    new = []
    for i in range(depth):
        st = (_conv_state_to_rows(conv_a[i]), _conv_state_to_rows(conv_b[i]),
              lru[i].reshape(n_groups, NB, -1), _conv_state_to_rows(conv_c[i]),
              _conv_state_to_rows(conv_d[i]), delta[i])
        p = {name: arr[i] for name, arr in params.items()}
        xc, xbc, (na, nb_, nl, nc, nd, ndelta) = _layer(xc, xbc, st, p, n_groups=n_groups, seq_len=l,
                                                        tt=tt, alpha=alpha)
        new.append((_conv_state_from_rows(na, b), _conv_state_from_rows(nb_, b), nl.reshape(b, -1),
                    _conv_state_from_rows(nc, b), _conv_state_from_rows(nd, b), ndelta))
    return _seq_from_rows(xc, b, l), tuple(jnp.stack(s) for s in zip(*new))


def kernel(x_prompt, x_sample, state_conv_a, state_conv_b, state_lru, state_conv_c, state_conv_d, state_delta, ln_in_g, ln_in_b, w_in, b_gate, a_conv_w, b_conv_w, b_conv_b, b_wx, b_bx, b_wa, b_ba, b_lambda, c_conv_w, c_conv_b, c_ln_g, c_ln_b, d_conv_w, d_a_log, d_dt_bias, d_norm_g, w_branch, w_out, ln_g, ln_b):
    depth, d, _ = w_in.shape
    w_a, w_b, w_c = a_conv_w.shape[2], b_conv_w.shape[2], c_conv_w.shape[2]
    w_qkv = d_conv_w.shape[2]
    n_heads, dv = d_a_log.shape[1], d_norm_g.shape[1]
    w_d = n_heads * dv
    assert n_heads <= LANES // 2
    off_b = 4 * w_a
    off_c = off_b + 2 * w_b
    off_d = off_c + 3 * w_c
    off_da = off_d + w_qkv + w_d
    off_g = off_da + 2 * n_heads
    alpha = (2 * depth) ** 0.25

    def row(a):
        return a.reshape(depth, 1, -1)

    def cols(lo, hi):
        if lo % (hi - lo) == 0 and (hi - lo) % LANES == 0:
            return _pack_rows(w_in, lo, hi - lo)
        return _pack_rows(w_in[:, :, lo:hi])

    d_par = jnp.zeros((depth, 8, LANES), f32)
    d_par = d_par.at[:, 0, :n_heads].set(d_a_log).at[:, 1, :n_heads].set(d_dt_bias)
    params = dict(
        w_a=cols(0, off_b), w_b=cols(off_b, off_c), w_c=cols(off_c, off_d),
        w_d=jnp.concatenate([cols(off_d, off_d + w_qkv), cols(off_d + w_qkv, off_da)], axis=-1),
        w_dab=jnp.pad(w_in[:, :, off_da:off_g], ((0, 0), (0, 0), (0, LANES - 2 * n_heads))).astype(bf16),
        w_g=cols(off_g, off_g + b_gate.shape[1] * d),
        b_gate=row(b_gate), a_conv_w=a_conv_w, b_conv_w=b_conv_w, b_conv_b=row(b_conv_b),
        b_wxa=jnp.concatenate([b_wx, b_wa], axis=-1).astype(bf16),
        b_bx=row(b_bx), b_ba=row(b_ba), b_lambda=row(b_lambda),
        c_conv_w=c_conv_w, c_conv_b=row(c_conv_b), c_ln_g=row(c_ln_g), c_ln_b=row(c_ln_b),
        d_conv_w=d_conv_w, d_par=d_par, d_norm_g=row(d_norm_g),
        w_branch=_pack_rows(w_branch.reshape((-1,) + w_branch.shape[2:])).reshape(
            w_branch.shape[:2] + (w_branch.shape[2] // 2, w_branch.shape[3])),
        w_out=_pack_rows(w_out), ln_g=row(ln_g), ln_b=row(ln_b),
    )
    g_in, b_in = ln_in_g.reshape(1, d), ln_in_b.reshape(1, d)

    bp = x_prompt.shape[0]

    def zeros(s):
        return jnp.zeros((depth, bp) + s.shape[2:], s.dtype)

    prompt_states = tuple(zeros(s) for s in (state_conv_a, state_conv_b, state_lru, state_conv_c,
                                              state_conv_d, state_delta))
    y_prompt, p_st = _run_trunk(x_prompt, prompt_states, g_in, b_in, params, depth, alpha)
    sample_states = (state_conv_a, state_conv_b, state_lru, state_conv_c, state_conv_d, state_delta)
    y_sample, s_st = _run_trunk(x_sample, sample_states, g_in, b_in, params, depth, alpha)
    return (y_prompt, y_sample) + tuple(p_st) + tuple(s_st)
```

```python
import functools

import jax
import jax.numpy as jnp
from jax import lax
from jax.experimental import pallas as pl
from jax.experimental.pallas import tpu as pltpu

f32 = jnp.float32
bf16 = jnp.bfloat16

LN_EPS = 1e-5
RMS_EPS = 1e-6
L2_EPS = 1e-6
LRU_C = 8.0
GDN_CHUNK = 64
NB = 8
LANES = 128
PACK = 2 * LANES
DELTA_SEQS = 4
TILE_FRAMES = 32
VMEM_LIMIT = 56 * 1024 * 1024


def _sigmoid(x):
    return 0.5 * jnp.tanh(0.5 * x) + 0.5


def _silu(x):
    h = 0.5 * x
    return h * jnp.tanh(h) + h


def _softplus(x):
    return jnp.maximum(x, 0.0) + jnp.log1p(jnp.exp(-jnp.abs(x)))


def _layer_norm(x, g, b):
    mu = jnp.mean(x, axis=-1, keepdims=True)
    xc = x - mu
    var = jnp.mean(xc * xc, axis=-1, keepdims=True)
    return xc * lax.rsqrt(var + LN_EPS) * g + b


def _dot(a, b):
    return jnp.dot(a, b, preferred_element_type=f32)


def _params(n_axes):
    return pltpu.CompilerParams(dimension_semantics=("arbitrary",) * n_axes,
                                vmem_limit_bytes=VMEM_LIMIT)


PACK_ROWS = 256


def _pack_rows_kernel(w_ref, o_ref):
    o_ref[...] = pltpu.bitcast(w_ref[...].astype(bf16), jnp.uint32)


def _pack_rows(w, lo=0, width=None):
    lead, k, n = w.shape
    width = n if width is None else width
    assert lo % width == 0 and lo + width <= n
    col = lo // width
    return pl.pallas_call(
        _pack_rows_kernel,
        grid=(lead, k // PACK_ROWS),
        in_specs=[pl.BlockSpec((None, PACK_ROWS, width), lambda l, i: (l, i, col))],
        out_specs=pl.BlockSpec((None, PACK_ROWS // 2, width), lambda l, i: (l, i, 0)),
        out_shape=jax.ShapeDtypeStruct((lead, k // 2, width), jnp.uint32),
        compiler_params=_params(2),
        name="pack_rows",
    )(w)


def _wdot(a, w_packed):
    return _dot(a, pltpu.bitcast(w_packed, bf16))


def _halo_init(cbuf, st_ref, t_idx):
    @pl.when(t_idx == 0)
    def _():
        cbuf[0:st_ref.shape[0], :] = st_ref[...]


def _causal_conv(cbuf, u, cw, halo):
    rows = u.shape[0]
    cbuf[halo:halo + rows, :] = u
    acc = None
    for k in range(cw.shape[0]):
        t = cbuf[k * NB:k * NB + rows, :] * cw[k:k + 1, :]
        acc = t if acc is None else acc + t
    return acc


def _halo_roll(cbuf, sto_ref, rows):
    halo = sto_ref.shape[0]
    new = cbuf[rows:rows + halo, :]
    cbuf[0:halo, :] = new
    sto_ref[...] = new


def _ln_in_kernel(x_ref, g_ref, b_ref, xo_ref, xb_ref):
    y = _layer_norm(x_ref[...], g_ref[...], b_ref[...])
    xo_ref[...] = y
    xb_ref[...] = y.astype(bf16)


def _ln_in(x, g, b, rows):
    t, d = x.shape
    tile = pl.BlockSpec((rows, d), lambda i: (i, 0))
    vec = pl.BlockSpec((1, d), lambda i: (0, 0))
    return pl.pallas_call(
        _ln_in_kernel,
        grid=(t // rows,),
        in_specs=[tile, vec, vec],
        out_specs=[tile, tile],
        out_shape=[jax.ShapeDtypeStruct((t, d), f32), jax.ShapeDtypeStruct((t, d), bf16)],
        compiler_params=_params(1),
        name="ln_in",
    )(x, g, b)


def _mix_in_kernel(xb_ref, wa_ref, wb_ref, wd_ref, wdab_ref, sta_ref, stb_ref, h0_ref, std_ref,
                   acw_ref, bcw_ref, bcb_ref, wxa_ref, bx_ref, ba_ref, lam_ref, dcw_ref, dpar_ref,
                   ya_ref, yb_ref, q_ref, k_ref, v_ref, z_ref, gb_ref,
                   stoa_ref, stob_ref, ho_ref, stod_ref,
                   cba, cbb, cbd, h_scr):
    t_idx = pl.program_id(1)
    rows, w = ya_ref.shape
    n_heads, _, dk = q_ref.shape
    n_blk, bw, _ = wxa_ref.shape
    _halo_init(cba, sta_ref, t_idx)
    _halo_init(cbb, stb_ref, t_idx)
    _halo_init(cbd, std_ref, t_idx)

    @pl.when(t_idx == 0)
    def _():
        h_scr[...] = h0_ref[...]

    xb = xb_ref[...]

    pa = _wdot(xb, wa_ref[...])
    a_b, a_c, a_x, a_z = (pa[:, n * w:(n + 1) * w] for n in range(4))
    conv = _causal_conv(cba, a_c * a_x, acw_ref[...], sta_ref.shape[0])
    ya_ref[...] = (a_b * conv * _silu(a_z)).astype(bf16)
    _halo_roll(cba, stoa_ref, rows)

    pb = _wdot(xb, wb_ref[...])
    xc = _causal_conv(cbb, pb[:, 0:w], bcw_ref[...], stb_ref.shape[0]) + bcb_ref[...]
    _halo_roll(cbb, stob_ref, rows)
    xc16 = xc.astype(bf16)
    pre = [_dot(xc16[:, h * bw:(h + 1) * bw], wxa_ref[h]) for h in range(n_blk)]
    gate_x = _sigmoid(jnp.concatenate([g[:, 0:bw] for g in pre], axis=1) + bx_ref[...])
    gate_a = _sigmoid(jnp.concatenate([g[:, bw:2 * bw] for g in pre], axis=1) + ba_ref[...])
    a = jnp.exp(gate_a * (-LRU_C * _softplus(-lam_ref[...])))
    bterm = jnp.sqrt(1.0 - a * a) * gate_x * xc
    h = h_scr[...]
    hs = []
    for s in range(rows // NB):
        h = a[s * NB:(s + 1) * NB, :] * h + bterm[s * NB:(s + 1) * NB, :]
        hs.append(h)
    h_scr[...] = h
    ho_ref[...] = h
    yb_ref[...] = (jnp.concatenate(hs, axis=0) * _silu(pb[:, w:2 * w])).astype(bf16)

    pd = _wdot(xb, wd_ref[...])
    pab = _dot(xb, wdab_ref[...])
    qkv = _silu(_causal_conv(cbd, pd[:, 0:3 * w], dcw_ref[...], std_ref.shape[0]))
    _halo_roll(cbd, stod_ref, rows)

    def l2n(xh, scale):
        ss = jnp.sum(xh * xh, axis=-1, keepdims=True)
        return xh * (lax.rsqrt(ss + L2_EPS) * scale)

    for hd in range(n_heads):
        q_ref[hd] = l2n(qkv[:, hd * dk:(hd + 1) * dk], dk ** -0.5)
        k_ref[hd] = l2n(qkv[:, w + hd * dk:w + (hd + 1) * dk], 1.0)
        v_ref[hd] = qkv[:, 2 * w + hd * dk:2 * w + (hd + 1) * dk]
        z_ref[hd] = _silu(pd[:, 3 * w + hd * dk:3 * w + (hd + 1) * dk])
    g = -jnp.exp(dpar_ref[0:1, :]) * _softplus(pab + dpar_ref[1:2, :])
    lane = lax.broadcasted_iota(jnp.int32, pab.shape, 1)
    gb_ref[...] = jnp.where(lane < n_heads, g, _sigmoid(pab))


def _split_bf16(x, parts):
    out = []
    for _ in range(parts):
        t = x.astype(bf16)
        out.append(t)
        x = x - t.astype(f32)
    return out


def _delta_kernel(q_ref, k_ref, v_ref, z_ref, gb_ref, s0_ref, ng_ref, y_ref, so_ref, s_scr, y_scr,
                  *, n_heads, seqs_per_iter):
    c_idx = pl.program_id(1)
    _, rows, dk = q_ref.shape
    c = rows // NB
    hg = PACK // c
    n_grp = n_heads // hg
    n_pair = n_heads // 2
    ppg = hg // 2
    spi = seqs_per_iter
    assert c * hg == PACK and n_grp * hg == n_heads and 2 * ppg == hg and 2 * dk == PACK

    @pl.when(c_idx == 0)
    def _():
        s_scr[...] = jnp.zeros(s_scr.shape, f32)
        for n in range(NB):
            for h in range(n_heads):
                o = (h % 2) * dk
                s_scr[n, h // 2, o:o + dk, o:o + dk] = s0_ref[n, h]

    def iota(shape, axis):
        return lax.broadcasted_iota(jnp.int32, shape, axis)

    def blk(i, size):
        return lax.shift_right_logical(i, size.bit_length() - 1)

    row = iota((c, PACK), 0)
    lane = iota((c, PACK), 1)
    jcol = lane & (c - 1)
    incl = row >= jcol
    strict = row > jcol
    eye_f = (row == jcol).astype(f32)
    head_sel = [(blk(lane, c) == hh).astype(bf16) for hh in range(hg)]
    base = blk(row, 8) == blk(jcol, 8)
    joins = []
    s = 8
    while s < c:
        joins.append((blk(row, 2 * s) == blk(jcol, 2 * s)) & (blk(row, s) != blk(jcol, s)))
        s *= 2
    ltri = (iota((c, c), 0) >= iota((c, c), 1)).astype(bf16)
    s_diag = blk(iota((PACK, PACK), 0), dk) == blk(iota((PACK, PACK), 1), dk)
    lane128 = iota((c, LANES), 1)
    is_g = lane128 < n_heads
    heads_per_tile = LANES // c
    zero16 = jnp.zeros((c, dk), bf16)
    ng = ng_ref[...]
    nt = (((1,), (1,)), ((), ()))
    tn = (((0,), (0,)), ((), ()))
    chains = [(i, g) for i in range(spi) for g in range(n_grp)]
    pairs = [(i, p) for i in range(spi) for p in range(n_pair)]

    def bd_heads(blocks):
        return jnp.concatenate(
            [jnp.concatenate([blocks[i] if i == j else zero16 for j in range(hg)], axis=1)
             for i in range(hg)], axis=0)

    def mmp(xs, ys):
        out = []
        for x, y in zip(xs, ys):
            y16 = y.astype(bf16)
            out.append(_dot(x.astype(bf16), jnp.concatenate([y16 * m for m in head_sel], axis=0)))
        return out

    def iteration(it, carry):
        seqs = [it * spi + i for i in range(spi)]
        sls = [pl.ds(n, c, stride=NB) for n in seqs]
        sq = range(spi)

        gbs = [gb_ref[sl, :] for sl in sls]
        cs = [_dot(ltri, jnp.concatenate(_split_bf16(jnp.where(is_g, gb, 0.0), 3), axis=1)) for gb in gbs]
        gc = [t[:, 0:LANES] + t[:, LANES:2 * LANES] + t[:, 2 * LANES:3 * LANES] for t in cs]

        def per_head_lanes(a, first):
            return jnp.concatenate([jnp.broadcast_to(a[:, first + h:first + h + 1], (c, dk))
                                    for h in range(n_heads)], axis=1)

        def packed_cols(full, g):
            tiles = []
            for t in range(PACK // LANES):
                first = g * hg + t * heads_per_tile
                val = full[:, first * dk:(first + 1) * dk]
                for m in range(1, heads_per_tile):
                    val = jnp.where(lane128 >= m * c, full[:, (first + m) * dk:(first + m + 1) * dk], val)
                tiles.append(val)
            return jnp.concatenate(tiles, axis=1)

        gc_full = [per_head_lanes(t, 0) for t in gc]
        beta_full = [per_head_lanes(t, n_heads) for t in gbs]
        gcol = [packed_cols(gc_full[i], g) for i, g in chains]
        grow = [jnp.sum(t * eye_f, axis=0, keepdims=True) for t in gcol]
        decay = [jnp.exp(jnp.where(incl, gcol[j] - grow[j], -jnp.inf)) for j in range(len(chains))]
        g_last = [t[c - 1:c, :] for t in gc_full]
        e_gc = [jnp.exp(t) for t in gc_full]
        e_rest = [jnp.exp(g_last[i] - gc_full[i]) for i in sq]
        e_last = [jnp.exp(t) for t in g_last]

        def heads16(a):
            return [a[:, h * dk:(h + 1) * dk].astype(bf16) for h in range(n_heads)]

        def load(ref, sl):
            return jnp.concatenate([ref[h, sl, :] for h in range(n_heads)], axis=1)

        qh = [heads16(load(q_ref, sl)) for sl in sls]
        k32 = [load(k_ref, sl) for sl in sls]
        kh = [heads16(t) for t in k32]
        kb32 = [k32[i] * beta_full[i] for i in sq]
        kbh = [heads16(t) for t in kb32]
        kbeh = [heads16(kb32[i] * e_gc[i]) for i in sq]
        vbh = [heads16(load(v_ref, sls[i]) * beta_full[i]) for i in sq]
        kbd = [bd_heads(kh[i][g * hg:(g + 1) * hg]) for i, g in chains]
        kq = [jnp.concatenate([jnp.concatenate(kbh[i][g * hg:(g + 1) * hg], axis=1),
                               jnp.concatenate(qh[i][g * hg:(g + 1) * hg], axis=1)], axis=0)
              for i, g in chains]
        kkqk = [lax.dot_general(kq[j], kbd[j], nt, preferred_element_type=f32) for j in range(len(chains))]
        a_mat = [jnp.where(strict, kkqk[j][0:c] * decay[j], 0.0) for j in range(len(chains))]
        p_mat = [(kkqk[j][c:2 * c] * decay[j]).astype(bf16) for j in range(len(chains))]

        a0 = [jnp.where(base, m, 0.0) for m in a_mat]
        a2 = mmp(a0, a0)
        x = [eye_f - m for m in a0]
        both = mmp([jnp.concatenate([p, q], axis=0) for p, q in zip(a2, x)], a2)
        a4 = [t[0:c] for t in both]
        x = [p + t[c:2 * c] for p, t in zip(x, both)]
        x = [p + q for p, q in zip(x, mmp(x, a4))]
        for join in joins:
            low = [jnp.where(join, m, 0.0) for m in a_mat]
            x = [p - q for p, q in zip(x, mmp(mmp(x, low), x))]

        def tile_of(arrs, i, p):
            lo = (p % ppg) * 2 * c
            return arrs[i * n_grp + p // ppg][:, lo // LANES * LANES:(lo // LANES + 1) * LANES]

        def pair_rows(p, row_a, row_b):
            first = ((p % ppg) * 2 * c % LANES) // c
            zeros = [zero16] * len(row_a)
            blocks = [row_a if r == first else row_b if r == first + 1 else zeros
                      for r in range(LANES // c)]
            return jnp.concatenate([jnp.concatenate(b, axis=1) for b in blocks], axis=0)

        x16 = [t.astype(bf16) for t in x]
        uw = [_dot(tile_of(x16, i, p),
                   pair_rows(p, [vbh[i][2 * p], zero16, kbeh[i][2 * p], zero16],
                             [zero16, vbh[i][2 * p + 1], zero16, kbeh[i][2 * p + 1]]))
              for i, p in pairs]
        s_old = [s_scr[seqs[i], p] for i, p in pairs]
        s16 = [t.astype(bf16) for t in s_old]
        k_p = [jnp.concatenate(kh[i][2 * p:2 * p + 2], axis=1) for i, p in pairs]
        wq = [_dot(jnp.concatenate([uw[m][:, PACK:2 * PACK].astype(bf16),
                                    jnp.concatenate(qh[i][2 * p:2 * p + 2], axis=1)], axis=0), s16[m])
              for m, (i, p) in enumerate(pairs)]
        u = [uw[m][:, 0:PACK] - wq[m][0:c] for m in range(len(pairs))]
        u16 = [t.astype(bf16) for t in u]
        pu = [_dot(tile_of(p_mat, i, p), pair_rows(p, [u16[m][:, 0:dk], zero16], [zero16, u16[m][:, dk:PACK]]))
              for m, (i, p) in enumerate(pairs)]
        ktu = [lax.dot_general(k_p[m], (u[m] * e_rest[i][:, p * PACK:(p + 1) * PACK]).astype(bf16), tn,
                               preferred_element_type=f32) for m, (i, p) in enumerate(pairs)]
        for m, (i, p) in enumerate(pairs):
            s_scr[seqs[i], p] = (s_old[m] * e_last[i][:, p * PACK:(p + 1) * PACK]
                                 + jnp.where(s_diag, ktu[m], 0.0))
            o_pair = wq[m][c:2 * c] * e_gc[i][:, p * PACK:(p + 1) * PACK] + pu[m]
            for hh in range(2):
                h = 2 * p + hh
                o = o_pair[:, hh * dk:(hh + 1) * dk]
                ms = jnp.mean(o * o, axis=-1, keepdims=True)
                y_scr[h, sls[i], :] = o * lax.rsqrt(ms + RMS_EPS) * ng * z_ref[h, sls[i], :]
        return carry

    lax.fori_loop(0, NB // spi, iteration, 0)
    for h in range(n_heads):
        y_ref[:, h * dk:(h + 1) * dk] = y_scr[h].astype(bf16)

    @pl.when(c_idx == pl.num_programs(1) - 1)
    def _():
        for n in range(NB):
            for h in range(n_heads):
                o = (h % 2) * dk
                so_ref[n, h] = s_scr[n, h // 2, o:o + dk, o:o + dk]


def _mix_out_kernel(x_ref, xb_ref, ya_ref, yb_ref, yd_ref, wc_ref, stc_ref, ccw_ref, ccb_ref, cg_ref,
                    cbeta_ref, wg_ref, bg_ref, wbr_ref, wo_ref, g_ref, b_ref,
                    xo_ref, xbo_ref, stoc_ref, cbc, *, alpha):
    t_idx = pl.program_id(1)
    rows, d = x_ref.shape
    w = ya_ref.shape[1]
    _halo_init(cbc, stc_ref, t_idx)
    xb = xb_ref[...]

    pc = _wdot(xb, wc_ref[...])
    graw = _wdot(xb, wg_ref[...])
    proj = {n: _wdot(y_ref[...], wbr_ref[n]) for n, y_ref in ((0, ya_ref), (1, yb_ref), (3, yd_ref))}

    glu = pc[:, 0:w] * _sigmoid(pc[:, w:2 * w])
    conv = _causal_conv(cbc, glu, ccw_ref[...], stc_ref.shape[0]) + ccb_ref[...]
    _halo_roll(cbc, stoc_ref, rows)
    y_c = (_silu(_layer_norm(conv, cg_ref[...], cbeta_ref[...])) * _silu(pc[:, 2 * w:3 * w])).astype(bf16)
    proj[2] = _wdot(y_c, wbr_ref[2])

    gates = _sigmoid(graw + bg_ref[...])
    mixed = None
    for n in range(4):
        t = gates[:, n * d:(n + 1) * d] * proj[n]
        mixed = t if mixed is None else mixed + t
    out = _wdot(mixed.astype(bf16), wo_ref[...])
    xn = _layer_norm(alpha * x_ref[...] + out, g_ref[...], b_ref[...])
    xo_ref[...] = xn
    xbo_ref[...] = xn.astype(bf16)


def _tile_spec(rows, width, n_t):
    return pl.BlockSpec((rows, width), lambda g, t: (g * n_t + t, 0))


def _group_spec(shape):
    n = len(shape)
    return pl.BlockSpec((None,) + tuple(shape), lambda g, t: (g,) + (0,) * n)


def _const_spec(shape):
    n = len(shape)
    return pl.BlockSpec(tuple(shape), lambda g, t: (0,) * n, pipeline_mode=pl.Buffered(1))


def _layer(x, xb, st, p, *, n_groups, seq_len, tt, alpha):
    conv_a, conv_b, lru, conv_c, conv_d, delta = st
    t_rows, d = x.shape
    w = p["a_conv_w"].shape[1]
    n_t = seq_len // tt
    rows = tt * NB
    grid = (n_groups, n_t)
    tile_in = _tile_spec(rows, d, n_t)
    tile_w = _tile_spec(rows, w, n_t)
    tile_gb = _tile_spec(rows, LANES, n_t)
    act16 = jax.ShapeDtypeStruct((t_rows, w), bf16)

    def shape_of(a):
        return jax.ShapeDtypeStruct(a.shape, f32)

    def grp(a):
        return _group_spec(a.shape[1:])

    def const(name):
        return _const_spec(p[name].shape)

    def cbuf(a):
        return pltpu.VMEM((a.shape[1] + rows, a.shape[2]), f32)

    n_heads, dk, dv = delta.shape[1:]
    assert dk == LANES and dv == LANES
    head_rows = jax.ShapeDtypeStruct((n_heads, t_rows, dk), f32)
    head_tile = pl.BlockSpec((n_heads, rows, dk), lambda g, t: (0, g * n_t + t, 0))
    in_names = ("w_a", "w_b", "w_d", "w_dab")
    par_names = ("a_conv_w", "b_conv_w", "b_conv_b", "b_wxa", "b_bx", "b_ba", "b_lambda", "d_conv_w",
                 "d_par")
    y_a, y_b, q, k, v, z, gb, new_a, new_b, new_lru, new_d = pl.pallas_call(
        _mix_in_kernel,
        grid=grid,
        in_specs=[tile_in] + [const(n) for n in in_names] + [grp(conv_a), grp(conv_b), grp(lru), grp(conv_d)]
        + [const(n) for n in par_names],
        out_specs=[tile_w, tile_w] + [head_tile] * 4 + [tile_gb, grp(conv_a), grp(conv_b), grp(lru),
                                                        grp(conv_d)],
        out_shape=[act16, act16] + [head_rows] * 4 + [jax.ShapeDtypeStruct((t_rows, LANES), f32),
                                                      shape_of(conv_a), shape_of(conv_b), shape_of(lru),
                                                      shape_of(conv_d)],
        scratch_shapes=[cbuf(conv_a), cbuf(conv_b), cbuf(conv_d), pltpu.VMEM(lru.shape[1:], f32)],
        compiler_params=_params(2),
        name="mix_in",
    )(xb, *[p[n] for n in in_names], conv_a, conv_b, lru, conv_d, *[p[n] for n in par_names])

    chunk = min(GDN_CHUNK, seq_len)
    n_c = seq_len // chunk
    c_rows = chunk * NB
    head_spec = pl.BlockSpec((n_heads, c_rows, dk), lambda g, c: (0, g * n_c + c, 0))
    state_spec = pl.BlockSpec((NB, n_heads, dk, dv), lambda g, c: (g, 0, 0, 0))
    y_d, new_delta = pl.pallas_call(
        functools.partial(_delta_kernel, n_heads=n_heads, seqs_per_iter=DELTA_SEQS),
        grid=(n_groups, n_c),
        in_specs=[head_spec, head_spec, head_spec, head_spec, _tile_spec(c_rows, LANES, n_c),
                  state_spec, const("d_norm_g")],
        out_specs=[_tile_spec(c_rows, w, n_c), state_spec],
        out_shape=[act16, shape_of(delta)],
        scratch_shapes=[pltpu.VMEM((NB, n_heads // 2, PACK, PACK), f32),
                        pltpu.VMEM((n_heads, c_rows, dk), f32)],
        compiler_params=_params(2),
        name="delta",
    )(q, k, v, z, gb, delta, p["d_norm_g"])

    out_names = ("c_conv_w", "c_conv_b", "c_ln_g", "c_ln_b", "w_g", "b_gate", "w_branch", "w_out",
                 "ln_g", "ln_b")
    x_new, xb_new, new_c = pl.pallas_call(
        functools.partial(_mix_out_kernel, alpha=alpha),
        grid=grid,
        in_specs=[tile_in, tile_in, tile_w, tile_w, tile_w, const("w_c"), grp(conv_c)]
        + [const(n) for n in out_names],
        out_specs=[tile_in, tile_in, grp(conv_c)],
        out_shape=[jax.ShapeDtypeStruct((t_rows, d), f32), jax.ShapeDtypeStruct((t_rows, d), bf16),
                   shape_of(conv_c)],
        scratch_shapes=[cbuf(conv_c)],
        compiler_params=_params(2),
        name="mix_out",
    )(x, xb, y_a, y_b, y_d, p["w_c"], conv_c, *[p[n] for n in out_names])

    return x_new, xb_new, (new_a, new_b, new_lru, new_c, new_d, new_delta)


def _rows_from_seq(x):
    b, l, d = x.shape
    return x.reshape(b // NB, NB, l, d).transpose(0, 2, 1, 3).reshape(b * l, d)


def _seq_from_rows(x, b, l):
    d = x.shape[-1]
    return x.reshape(b // NB, l, NB, d).transpose(0, 2, 1, 3).reshape(b, l, d)


def _conv_state_to_rows(s):
    b, k1, c = s.shape
    return s.reshape(b // NB, NB, k1, c).transpose(0, 2, 1, 3).reshape(b // NB, k1 * NB, c)


def _conv_state_from_rows(s, b):
    g, r, c = s.shape
    k1 = r // NB
    return s.reshape(g, k1, NB, c).transpose(0, 2, 1, 3).reshape(b, k1, c)


def _run_trunk(x, states, ln_in_g, ln_in_b, params, depth, alpha):
    b, l, d = x.shape
    assert b % NB == 0
    n_groups = b // NB
    tt = min(l, TILE_FRAMES)
    assert l % tt == 0 and l % min(GDN_CHUNK, l) == 0
    conv_a, conv_b, lru, conv_c, conv_d, delta = states
    rows = _rows_from_seq(x)
    ln_rows = tt * NB * (4 if (b * l) % (4 * tt * NB) == 0 else 1)
    xc, xbc = _ln_in(rows, ln_in_g, ln_in_b, ln_rows)
    new = []
    for i in range(depth):
        st = (_conv_state_to_rows(conv_a[i]), _conv_state_to_rows(conv_b[i]),
              lru[i].reshape(n_groups, NB, -1), _conv_state_to_rows(conv_c[i]),
              _conv_state_to_rows(conv_d[i]), delta[i])
        p = {name: arr[i] for name, arr in params.items()}
        xc, xbc, (na, nb_, nl, nc, nd, ndelta) = _layer(xc, xbc, st, p, n_groups=n_groups, seq_len=l,
                                                        tt=tt, alpha=alpha)
        new.append((_conv_state_from_rows(na, b), _conv_state_from_rows(nb_, b), nl.reshape(b, -1),
                    _conv_state_from_rows(nc, b), _conv_state_from_rows(nd, b), ndelta))
    return _seq_from_rows(xc, b, l), tuple(jnp.stack(s) for s in zip(*new))


def kernel(x_prompt, x_sample, state_conv_a, state_conv_b, state_lru, state_conv_c, state_conv_d, state_delta, ln_in_g, ln_in_b, w_in, b_gate, a_conv_w, b_conv_w, b_conv_b, b_wx, b_bx, b_wa, b_ba, b_lambda, c_conv_w, c_conv_b, c_ln_g, c_ln_b, d_conv_w, d_a_log, d_dt_bias, d_norm_g, w_branch, w_out, ln_g, ln_b):
    depth, d, _ = w_in.shape
    w_a, w_b, w_c = a_conv_w.shape[2], b_conv_w.shape[2], c_conv_w.shape[2]
    w_qkv = d_conv_w.shape[2]
    n_heads, dv = d_a_log.shape[1], d_norm_g.shape[1]
    w_d = n_heads * dv
    assert n_heads <= LANES // 2
    off_b = 4 * w_a
    off_c = off_b + 2 * w_b
    off_d = off_c + 3 * w_c
    off_da = off_d + w_qkv + w_d
    off_g = off_da + 2 * n_heads
    alpha = (2 * depth) ** 0.25

    def row(a):
        return a.reshape(depth, 1, -1)

    def cols(lo, hi):
        if lo % (hi - lo) == 0 and (hi - lo) % LANES == 0:
            return _pack_rows(w_in, lo, hi - lo)
        return _pack_rows(w_in[:, :, lo:hi])

    d_par = jnp.zeros((depth, 8, LANES), f32)
    d_par = d_par.at[:, 0, :n_heads].set(d_a_log).at[:, 1, :n_heads].set(d_dt_bias)
    params = dict(
        w_a=cols(0, off_b), w_b=cols(off_b, off_c), w_c=cols(off_c, off_d),
        w_d=jnp.concatenate([cols(off_d, off_d + w_qkv), cols(off_d + w_qkv, off_da)], axis=-1),
        w_dab=jnp.pad(w_in[:, :, off_da:off_g], ((0, 0), (0, 0), (0, LANES - 2 * n_heads))).astype(bf16),
        w_g=cols(off_g, off_g + b_gate.shape[1] * d),
        b_gate=row(b_gate), a_conv_w=a_conv_w, b_conv_w=b_conv_w, b_conv_b=row(b_conv_b),
        b_wxa=jnp.concatenate([b_wx, b_wa], axis=-1).astype(bf16),
        b_bx=row(b_bx), b_ba=row(b_ba), b_lambda=row(b_lambda),
        c_conv_w=c_conv_w, c_conv_b=row(c_conv_b), c_ln_g=row(c_ln_g), c_ln_b=row(c_ln_b),
        d_conv_w=d_conv_w, d_par=d_par, d_norm_g=row(d_norm_g),
        w_branch=_pack_rows(w_branch.reshape((-1,) + w_branch.shape[2:])).reshape(
            w_branch.shape[:2] + (w_branch.shape[2] // 2, w_branch.shape[3])),
        w_out=_pack_rows(w_out), ln_g=row(ln_g), ln_b=row(ln_b),
    )
    g_in, b_in = ln_in_g.reshape(1, d), ln_in_b.reshape(1, d)

    bp = x_prompt.shape[0]

    def zeros(s):
        return jnp.zeros((depth, bp) + s.shape[2:], s.dtype)

    prompt_states = tuple(zeros(s) for s in (state_conv_a, state_conv_b, state_lru, state_conv_c,
                                              state_conv_d, state_delta))
    y_prompt, p_st = _run_trunk(x_prompt, prompt_states, g_in, b_in, params, depth, alpha)
    sample_states = (state_conv_a, state_conv_b, state_lru, state_conv_c, state_conv_d, state_delta)
    y_sample, s_st = _run_trunk(x_sample, sample_states, g_in, b_in, params, depth, alpha)
    return (y_prompt, y_sample) + tuple(p_st) + tuple(s_st)
```
